```python
import jax
import jax.numpy as jnp
from jax import lax
import numpy as np

D_MODEL = 2048
BATCH = 2
SEQ = 16384
DEPTH = 4

GRID_W = 64
CTX_LEN = 256
ATTN_WIDTH = D_MODEL // 2
FOURIER_WIDTH = D_MODEL - ATTN_WIDTH
V_HEAD_DIM = 128
N_HEADS = ATTN_WIDTH // V_HEAD_DIM
QK_NOPE_DIM = 128
QK_ROPE_DIM = 64
QK_DIM = QK_NOPE_DIM + QK_ROPE_DIM
Q_LORA_RANK = 512
KV_LORA_RANK = 256
FOURIER_GROUPS = 4
FOURIER_GROUP_CH = FOURIER_WIDTH // FOURIER_GROUPS
IDX_Q = Q_LORA_RANK
IDX_R = IDX_Q + KV_LORA_RANK + QK_ROPE_DIM
IN_COLS = IDX_R + FOURIER_WIDTH
D_FF = 5632
N_MOD = 9
AXIS_FREQS = QK_ROPE_DIM // 4
ROPE_THETA = 10000.0
Q_BLOCK = 128
SM_SCALE = QK_DIM ** -0.5
RMS_EPS = 1e-6

kernel_name = "hymba_mla_fnet_macaron_dit"


def rms_norm(x, g, eps=RMS_EPS):
    xf = x.astype(jnp.float32)
    y = xf * lax.rsqrt(jnp.mean(xf * xf, axis=-1, keepdims=True) + eps)
    return (y * g.astype(jnp.float32)).astype(x.dtype)


def adaln(cond, w, b):
    m = jax.nn.silu(cond) @ w + b
    return jnp.split(m[..., None, :], N_MOD, axis=-1)


def modulate(h, shift, scale):
    return h * (1.0 + scale) + shift


def swiglu(h, w_gate, w_up, w_down):
    return (jax.nn.silu(h @ w_gate) * (h @ w_up)) @ w_down


def axial_rope_tables(n_tokens, dtype):
    rows = n_tokens // GRID_W
    row = jnp.broadcast_to(jnp.arange(rows, dtype=jnp.float32)[:, None], (rows, GRID_W)).reshape(-1)
    col = jnp.broadcast_to(jnp.arange(GRID_W, dtype=jnp.float32)[None, :], (rows, GRID_W)).reshape(-1)
    inv_freq = ROPE_THETA ** (-jnp.arange(AXIS_FREQS, dtype=jnp.float32) / AXIS_FREQS)
    ang_r = row[:, None] * inv_freq
    ang_c = col[:, None] * inv_freq
    return tuple(t.astype(dtype) for t in (jnp.cos(ang_r), jnp.sin(ang_r), jnp.cos(ang_c), jnp.sin(ang_c)))


def rotate_half_split(x, cos, sin):
    x1, x2 = jnp.split(x, 2, axis=-1)
    cos = cos[None, :, None, :]
    sin = sin[None, :, None, :]
    return jnp.concatenate([x1 * cos - x2 * sin, x1 * sin + x2 * cos], axis=-1)


def apply_axial_rope(x, rope):
    cos_r, sin_r, cos_c, sin_c = rope
    xr, xc = jnp.split(x, 2, axis=-1)
    return jnp.concatenate([rotate_half_split(xr, cos_r, sin_r), rotate_half_split(xc, cos_c, sin_c)], axis=-1)


def with_rope(t, rope):
    return jnp.concatenate([t[..., :QK_NOPE_DIM], apply_axial_rope(t[..., QK_NOPE_DIM:], rope)], axis=-1)


def mla_queries(q_lat, g_lat, w_uq, g_qn, rope):
    B, L, _ = q_lat.shape
    q = (rms_norm(q_lat, g_lat) @ w_uq).reshape(B, L, N_HEADS, QK_DIM)
    q = rms_norm(q, g_qn)
    return q if rope is None else with_rope(q, rope)


def mla_keys_values(kv_part, g_lat, w_ukv, g_kn, rope):
    B, L, _ = kv_part.shape
    kv_lat, k_rope = kv_part[..., :KV_LORA_RANK], kv_part[..., KV_LORA_RANK:]
    kv = (rms_norm(kv_lat, g_lat) @ w_ukv).reshape(B, L, N_HEADS, QK_NOPE_DIM + V_HEAD_DIM)
    k_nope, v = kv[..., :QK_NOPE_DIM], kv[..., QK_NOPE_DIM:]
    k_rope = jnp.broadcast_to(k_rope[:, :, None, :], (B, L, N_HEADS, QK_ROPE_DIM))
    k = rms_norm(jnp.concatenate([k_nope, k_rope], axis=-1), g_kn)
    return (k if rope is None else with_rope(k, rope)), v


def attend(q, k, v):
    s = jnp.einsum('bqhd,bkhd->bhqk', q, k, preferred_element_type=jnp.float32) * SM_SCALE
    p = jax.nn.softmax(s, axis=-1).astype(v.dtype)
    return jnp.einsum('bhqk,bkhd->bqhd', p, v)


def latent_attention(q, k, v):
    B, S, H, Dq = q.shape
    n_blk = S // Q_BLOCK
    qb = q.reshape(B, n_blk, Q_BLOCK, H, Dq).transpose(1, 0, 2, 3, 4)
    o = lax.map(lambda qi: attend(qi, k, v), qb)
    return o.transpose(1, 0, 2, 3, 4).reshape(B, S, H * V_HEAD_DIM)


def fourier_mix(f, w_f):
    B, L, _ = f.shape
    fg = f.reshape(B, L, FOURIER_GROUPS, FOURIER_GROUP_CH).transpose(0, 2, 1, 3).astype(jnp.float32)
    mixed = jnp.fft.fft2(fg, norm="ortho").real.astype(f.dtype)
    out = jnp.einsum('bglc,gcd->blgd', mixed, w_f)
    return out.reshape(B, L, FOURIER_WIDTH)


def setup_inputs(seed: int = 0) -> dict:
    key = jax.random.key(seed)
    ks = jax.random.split(key, 24)
    f32 = jnp.float32

    def nrm(k, shape, scale):
        return jax.random.normal(k, shape, f32) * scale

    def gain(k, shape):
        return 1.0 + 0.02 * jax.random.normal(k, shape, f32)

    return {
        "x": nrm(ks[0], (BATCH, SEQ, D_MODEL), 1.0),
        "c": nrm(ks[1], (BATCH, D_MODEL), 1.0),
        "ctx": nrm(ks[2], (BATCH, CTX_LEN, D_MODEL), 1.0),
        "c_ctx": nrm(ks[3], (D_MODEL,), 1.0),
        "w_ada": nrm(ks[4], (DEPTH, D_MODEL, N_MOD * D_MODEL), 0.5 * D_MODEL ** -0.5),
        "b_ada": nrm(ks[5], (DEPTH, N_MOD * D_MODEL), 0.02),
        "norm_g": gain(ks[6], (DEPTH, 3, D_MODEL)),
        "ffn1_w_gate": nrm(ks[7], (DEPTH, D_MODEL, D_FF), D_MODEL ** -0.5),
        "ffn1_w_up": nrm(ks[8], (DEPTH, D_MODEL, D_FF), D_MODEL ** -0.5),
        "ffn1_w_down": nrm(ks[9], (DEPTH, D_FF, D_MODEL), D_FF ** -0.5),
        "ffn2_w_gate": nrm(ks[10], (DEPTH, D_MODEL, D_FF), D_MODEL ** -0.5),
        "ffn2_w_up": nrm(ks[11], (DEPTH, D_MODEL, D_FF), D_MODEL ** -0.5),
        "ffn2_w_down": nrm(ks[12], (DEPTH, D_FF, D_MODEL), D_FF ** -0.5),
        "w_in": nrm(ks[13], (DEPTH, D_MODEL, IN_COLS), D_MODEL ** -0.5),
        "q_lat_g": gain(ks[14], (DEPTH, Q_LORA_RANK)),
        "w_uq": nrm(ks[15], (DEPTH, Q_LORA_RANK, N_HEADS * QK_DIM), Q_LORA_RANK ** -0.5),
        "kv_lat_g": gain(ks[16], (DEPTH, KV_LORA_RANK)),
        "w_ukv": nrm(ks[17], (DEPTH, KV_LORA_RANK, N_HEADS * (QK_NOPE_DIM + V_HEAD_DIM)), KV_LORA_RANK ** -0.5),
        "q_norm_g": gain(ks[18], (DEPTH, QK_DIM)),
        "k_norm_g": gain(ks[19], (DEPTH, QK_DIM)),
        "w_fourier": nrm(ks[20], (DEPTH, FOURIER_GROUPS, FOURIER_GROUP_CH, FOURIER_GROUP_CH), FOURIER_GROUP_CH ** -0.5),
        "w_o": nrm(ks[21], (DEPTH, ATTN_WIDTH + FOURIER_WIDTH, D_MODEL), (ATTN_WIDTH + FOURIER_WIDTH) ** -0.5),
    }


def reference(x, c, ctx, c_ctx, w_ada, b_ada, norm_g,
              ffn1_w_gate, ffn1_w_up, ffn1_w_down, ffn2_w_gate, ffn2_w_up, ffn2_w_down,
              w_in, q_lat_g, w_uq, kv_lat_g, w_ukv, q_norm_g, k_norm_g, w_fourier, w_o):
    rope = axial_rope_tables(x.shape[1], x.dtype)
    xc = ctx
    for l in range(DEPTH):
        last = l == DEPTH - 1
        mod = adaln(c, w_ada[l], b_ada[l])
        modc = adaln(c_ctx, w_ada[l], b_ada[l])
        ffn1 = (ffn1_w_gate[l], ffn1_w_up[l], ffn1_w_down[l])
        ffn2 = (ffn2_w_gate[l], ffn2_w_up[l], ffn2_w_down[l])

        x = x + 0.5 * mod[2] * swiglu(modulate(rms_norm(x, norm_g[l, 0]), mod[0], mod[1]), *ffn1)
        xc = xc + 0.5 * modc[2] * swiglu(modulate(rms_norm(xc, norm_g[l, 0]), modc[0], modc[1]), *ffn1)

        h = modulate(rms_norm(x, norm_g[l, 1]), mod[3], mod[4])
        hc = modulate(rms_norm(xc, norm_g[l, 1]), modc[3], modc[4])
        proj = h @ w_in[l]
        q = mla_queries(proj[..., :IDX_Q], q_lat_g[l], w_uq[l], q_norm_g[l], rope)
        k, v = mla_keys_values(proj[..., IDX_Q:IDX_R], kv_lat_g[l], w_ukv[l], k_norm_g[l], rope)
        if last:
            kc, vc = mla_keys_values(hc @ w_in[l][:, IDX_Q:IDX_R], kv_lat_g[l], w_ukv[l], k_norm_g[l], None)
        else:
            proj_c = hc @ w_in[l]
            kc, vc = mla_keys_values(proj_c[..., IDX_Q:IDX_R], kv_lat_g[l], w_ukv[l], k_norm_g[l], None)

        attn = latent_attention(q, jnp.concatenate([k, kc], axis=1), jnp.concatenate([v, vc], axis=1))
        four = fourier_mix(proj[..., IDX_R:], w_fourier[l])
        x = x + mod[5] * (jnp.concatenate([attn, four], axis=-1) @ w_o[l])

        x = x + 0.5 * mod[8] * swiglu(modulate(rms_norm(x, norm_g[l, 2]), mod[6], mod[7]), *ffn2)

        if not last:
            qc = mla_queries(proj_c[..., :IDX_Q], q_lat_g[l], w_uq[l], q_norm_g[l], None)
            Bc, Lc = qc.shape[0], qc.shape[1]
            attn_c = attend(qc, kc, vc).reshape(Bc, Lc, ATTN_WIDTH)
            four_c = fourier_mix(proj_c[..., IDX_R:], w_fourier[l])
            xc = xc + modc[5] * (jnp.concatenate([attn_c, four_c], axis=-1) @ w_o[l])
            xc = xc + 0.5 * modc[8] * swiglu(modulate(rms_norm(xc, norm_g[l, 2]), modc[6], modc[7]), *ffn2)
    return x
```

```python
import functools
import math

import jax
import jax.numpy as jnp
import numpy as np
from jax import lax
from jax.experimental import pallas as pl
from jax.experimental.pallas import tpu as pltpu

GRID_W = 64
V_HEAD_DIM = 128
QK_NOPE_DIM = 128
QK_ROPE_DIM = 64
QK_DIM = QK_NOPE_DIM + QK_ROPE_DIM
FOURIER_GROUPS = 4
N_MOD = 9
AXIS_FREQS = QK_ROPE_DIM // 4
ROPE_THETA = 10000.0
SM_SCALE = QK_DIM ** -0.5
RMS_EPS = 1e-6
LOG2E = 1.4426950408889634

LANES = 128
COND_ROWS = 8
VMEM_LIMIT = 56 * 1024 * 1024

ROW_TILE = 512
KEY_CHUNK = 256
FFN_TILE = 512
ADA_TILE = 1024
ATT_TQ = 512
ATT_TK = 512

BF16 = jnp.bfloat16
F32 = jnp.float32


def _dot(a, b):
    return jnp.dot(a, b, preferred_element_type=F32)


def _dot_nt(a, b):
    return lax.dot_general(a, b, (((1,), (1,)), ((), ())), preferred_element_type=F32)


def _cparams(sem):
    return pltpu.CompilerParams(dimension_semantics=sem, vmem_limit_bytes=VMEM_LIMIT)


def _norm_mod(x, g, shift, scale):
    y = x * lax.rsqrt(jnp.mean(x * x, axis=-1, keepdims=True) + RMS_EPS) * g
    return y * (1.0 + scale) + shift


def _adaln_kernel(c_ref, w_ref, b_ref, o_ref):
    c = c_ref[...]
    s = (c * jax.nn.sigmoid(c)).astype(BF16)
    o_ref[...] = _dot(s, w_ref[...].astype(BF16)) + b_ref[...]


def _adaln(cond, w_ada, b_ada):
    depth, d, n = w_ada.shape
    tn = math.gcd(ADA_TILE, d)
    return pl.pallas_call(
        _adaln_kernel,
        grid=(depth, n // tn),
        in_specs=[
            pl.BlockSpec((COND_ROWS, d), lambda l, j: (0, 0)),
            pl.BlockSpec((None, d, tn), lambda l, j: (l, 0, j)),
            pl.BlockSpec((None, 1, tn), lambda l, j: (l, 0, j)),
        ],
        out_specs=pl.BlockSpec((None, COND_ROWS, tn), lambda l, j: (l, 0, j)),
        out_shape=jax.ShapeDtypeStruct((depth, COND_ROWS, n), F32),
        compiler_params=_cparams(("arbitrary", "arbitrary")),
        name="adaln",
    )(cond, w_ada, b_ada.reshape(depth, 1, n))


def _ffn_kernel(x_ref, mod_ref, g_ref, wg_ref, wu_ref, wd_ref, o_ref, h_ref, acc_ref, *, mod0):
    f = pl.program_id(1)

    @pl.when(f == 0)
    def _():
        h = _norm_mod(x_ref[...], g_ref[...], mod_ref[mod0:mod0 + 1, :], mod_ref[mod0 + 1:mod0 + 2, :])
        h_ref[...] = h.astype(BF16)
        acc_ref[...] = jnp.zeros_like(acc_ref)

    h = h_ref[...]
    gate = _dot(h, wg_ref[...])
    up = _dot(h, wu_ref[...])
    a = (gate * jax.nn.sigmoid(gate) * up).astype(BF16)
    acc_ref[...] += _dot(a, wd_ref[...])

    @pl.when(f == pl.num_programs(1) - 1)
    def _():
        o_ref[...] = x_ref[...] + (0.5 * mod_ref[mod0 + 2:mod0 + 3, :]) * acc_ref[...]


def _ffn(xa, mods, g, wg, wu, wd, layer, *, mod0, rows, seq):
    d = xa.shape[1]
    d_ff = wg.shape[2]
    tm, tf = ROW_TILE, FFN_TILE
    row = lambda i, f: (i, 0)
    return pl.pallas_call(
        functools.partial(_ffn_kernel, mod0=mod0),
        grid=(rows // tm, d_ff // tf),
        in_specs=[
            pl.BlockSpec((tm, d), row),
            pl.BlockSpec((None, N_MOD, d), lambda i, f: ((i * tm) // seq, 0, 0)),
            pl.BlockSpec((1, d), lambda i, f: (0, 0)),
            pl.BlockSpec((None, d, tf), lambda i, f: (layer, 0, f)),
            pl.BlockSpec((None, d, tf), lambda i, f: (layer, 0, f)),
            pl.BlockSpec((None, tf, d), lambda i, f: (layer, f, 0)),
        ],
        out_specs=pl.BlockSpec((tm, d), row),
        out_shape=jax.ShapeDtypeStruct((rows, d), F32),
        scratch_shapes=[pltpu.VMEM((tm, d), BF16), pltpu.VMEM((tm, d), F32)],
        compiler_params=_cparams(("parallel", "arbitrary")),
        name="ffn",
    )(xa, mods, g, wg, wu, wd)


def _mixer_in_kernel(x_ref, mod_ref, g_ref, win_ref, gql_ref, wuqt_ref, gq_ref, gkvl_ref, wuk_ref, wuvt_ref,
                     gkn_ref, gkr_ref, gkrs_ref, ck_ref, sk_ref, cq_ref, sq_ref, dft_ref,
                     qt_ref, k_ref, vt_ref, zr_ref, zi_ref, *, heads, rq, rkv, fw, groups):
    tm = x_ref.shape[0]
    h = _norm_mod(x_ref[...], g_ref[...], mod_ref[3:4, :], mod_ref[4:5, :]).astype(BF16)
    proj = _dot(h, win_ref[...])
    f0 = rq + rkv
    r0 = f0 + fw

    qlat = proj[:, :rq]
    qn = (qlat * lax.rsqrt(jnp.mean(qlat * qlat, axis=-1, keepdims=True) + RMS_EPS) * gql_ref[...]).astype(BF16)
    qt = _dot_nt(wuqt_ref[...], qn)
    gq = gq_ref[...]
    cq = cq_ref[...]
    sq = sq_ref[...]
    a = AXIS_FREQS
    for hd in range(heads):
        blk = qt[hd * QK_DIM:(hd + 1) * QK_DIM]
        r = lax.rsqrt(jnp.mean(blk * blk, axis=0, keepdims=True) + RMS_EPS) * (SM_SCALE * LOG2E)
        blk = blk * r * gq
        rp = blk[QK_NOPE_DIM:]
        sw = jnp.concatenate([rp[a:2 * a], rp[:a], rp[3 * a:], rp[2 * a:3 * a]], axis=0)
        qt_ref[hd, :QK_NOPE_DIM, :] = blk[:QK_NOPE_DIM].astype(BF16)
        qt_ref[hd, QK_NOPE_DIM:, :] = (rp * cq + sw * sq).astype(BF16)

    kvlat = proj[:, rq:f0]
    kvn = (kvlat * lax.rsqrt(jnp.mean(kvlat * kvlat, axis=-1, keepdims=True) + RMS_EPS) * gkvl_ref[...]).astype(BF16)
    knope = _dot(kvn, wuk_ref[...])
    vt = _dot_nt(wuvt_ref[...], kvn)
    for c in range(tm // KEY_CHUNK):
        piece = vt[:, c * KEY_CHUNK:(c + 1) * KEY_CHUNK].astype(BF16)
        vt_ref[:, c] = piece.reshape(heads, V_HEAD_DIM, KEY_CHUNK)
    kr = proj[:, r0:r0 + LANES]
    krs = proj[:, r0 + LANES:r0 + 2 * LANES]
    krr = kr * gkr_ref[...] * ck_ref[...] + krs * gkrs_ref[...] * sk_ref[...]
    ss_r = jnp.sum(kr * kr, axis=-1, keepdims=True)
    gkn = gkn_ref[...]
    for hd in range(heads):
        kn = knope[:, hd * QK_NOPE_DIM:(hd + 1) * QK_NOPE_DIM]
        ss = jnp.sum(kn * kn, axis=-1, keepdims=True) + ss_r
        r = lax.rsqrt(ss * (1.0 / QK_DIM) + RMS_EPS)
        k_ref[hd, :, :QK_NOPE_DIM] = (kn * r * gkn).astype(BF16)
        k_ref[hd, :, QK_NOPE_DIM:] = (krr * r)[:, :QK_ROPE_DIM].astype(BF16)

    ch = fw // groups
    fb = proj[:, f0:r0].astype(BF16)
    dft = dft_ref[...]
    for gi in range(groups):
        uw = _dot(fb[:, gi * ch:(gi + 1) * ch], dft)
        zr_ref[:, gi * ch:(gi + 1) * ch] = uw[:, :ch].astype(BF16)
        zi_ref[:, gi * ch:(gi + 1) * ch] = uw[:, ch:].astype(BF16)


def _mixer_in(xa, mods, g, p, layer, *, seq, lat_rows, dims):
    rows, d = xa.shape
    heads, rq, rkv, fw, groups = dims
    tm = ROW_TILE
    ncols = p["w_in"].shape[2]
    n_lat = lat_rows // tm
    per_seq = seq // tm
    const2 = lambda i: (0, 0)
    row = lambda i: (i, 0)
    pos = lambda i: (jnp.where(i < n_lat, i % per_seq, per_seq), 0)
    post = lambda i: (0, jnp.where(i < n_lat, i % per_seq, per_seq))
    lay = lambda *shape: pl.BlockSpec((None,) + shape, lambda i: (layer,) + (0,) * len(shape),
                                      pipeline_mode=pl.Buffered(1))
    ch = fw // groups
    out_shape = [
        jax.ShapeDtypeStruct((heads, QK_DIM, rows), BF16),
        jax.ShapeDtypeStruct((heads, rows, QK_DIM), BF16),
        jax.ShapeDtypeStruct((heads, rows // KEY_CHUNK, V_HEAD_DIM, KEY_CHUNK), BF16),
        jax.ShapeDtypeStruct((rows, fw), BF16),
        jax.ShapeDtypeStruct((rows, fw), BF16),
    ]
    out_specs = [
        pl.BlockSpec((heads, QK_DIM, tm), lambda i: (0, 0, i)),
        pl.BlockSpec((heads, tm, QK_DIM), lambda i: (0, i, 0)),
        pl.BlockSpec((heads, tm // KEY_CHUNK, V_HEAD_DIM, KEY_CHUNK), lambda i: (0, i, 0, 0)),
        pl.BlockSpec((tm, fw), row),
        pl.BlockSpec((tm, fw), row),
    ]
    in_specs = [
        pl.BlockSpec((tm, d), row),
        pl.BlockSpec((None, N_MOD, d), lambda i: ((i * tm) // seq, 0, 0)),
        pl.BlockSpec((1, d), const2),
        lay(d, ncols),
        lay(1, rq),
        lay(heads * QK_DIM, rq),
        lay(QK_DIM, 1),
        lay(1, rkv),
        lay(rkv, heads * QK_NOPE_DIM),
        lay(heads * V_HEAD_DIM, rkv),
        lay(1, LANES), lay(1, LANES), lay(1, LANES),
        pl.BlockSpec((tm, LANES), pos), pl.BlockSpec((tm, LANES), pos),
        pl.BlockSpec((QK_ROPE_DIM, tm), post), pl.BlockSpec((QK_ROPE_DIM, tm), post),
        pl.BlockSpec((ch, 2 * ch), const2),
    ]
    return pl.pallas_call(
        functools.partial(_mixer_in_kernel, heads=heads, rq=rq, rkv=rkv, fw=fw, groups=groups),
        grid=(rows // tm,),
        in_specs=in_specs,
        out_specs=out_specs,
        out_shape=out_shape,
        compiler_params=_cparams(("parallel",)),
        name="mixer_in",
    )(xa, mods, g, p["w_in"], p["q_lat_g"], p["w_uqt"], p["gq"], p["kv_lat_g"], p["w_uk"], p["w_uvt"],
      p["gk_n"], p["gk_r"], p["gk_rs"], p["rope_ck"], p["rope_sk"], p["rope_cqt"], p["rope_sqt"], p["dft_ch"])


def _dft_a_kernel(zr_ref, zi_ref, fa_ref, tc_ref, ts_ref, o_ref, *, fw):
    n1 = zr_ref.shape[0]
    fa = fa_ref[...]
    reps = fw // LANES
    for t in range(zr_ref.shape[1] // fw):
        cols = slice(t * fw, (t + 1) * fw)
        zst = jnp.concatenate([zr_ref[:, cols], zi_ref[:, cols]], axis=0)
        z1 = _dot(fa, zst)
        z1r, z1i = z1[:n1], z1[n1:]
        tc = jnp.concatenate([tc_ref[t]] * reps, axis=1)
        ts = jnp.concatenate([ts_ref[t]] * reps, axis=1)
        o_ref[0, t] = (z1r * tc + z1i * ts).astype(BF16)
        o_ref[1, t] = (z1i * tc - z1r * ts).astype(BF16)


def _dft_a(zr, zi, p, *, batch, n1, n2, fw):
    rows = zr.shape[0]
    cb = 8
    zr2 = zr.reshape(rows // n2, n2 * fw)
    zi2 = zi.reshape(rows // n2, n2 * fw)
    blk = pl.BlockSpec((n1, cb * fw), lambda b, j: (b, j))
    tw = pl.BlockSpec((cb, n1, LANES), lambda b, j: (j, 0, 0))
    return pl.pallas_call(
        functools.partial(_dft_a_kernel, fw=fw),
        grid=(batch, n2 // cb),
        in_specs=[blk, blk, pl.BlockSpec((2 * n1, 2 * n1), lambda b, j: (0, 0)), tw, tw],
        out_specs=pl.BlockSpec((None, 2, cb, n1, fw), lambda b, j: (b, 0, j, 0, 0)),
        out_shape=jax.ShapeDtypeStruct((batch, 2, n2, n1, fw), BF16),
        compiler_params=_cparams(("parallel", "parallel")),
        name="dft_rows_a",
    )(zr2, zi2, p["dft_a"], p["tw_c"], p["tw_s"])


def _dft_b_kernel(z_ref, fb_ref, o_ref):
    n2 = z_ref.shape[1]
    zst = z_ref[...].reshape(2 * n2, z_ref.shape[2])
    o_ref[...] = _dot(fb_ref[...], zst).astype(BF16)


def _dft_b(z2, p, *, rows, batch, n1, n2, fw):
    wc = 8 * fw
    z2v = z2.reshape(batch, 2, n2, n1 * fw)
    y = pl.pallas_call(
        _dft_b_kernel,
        grid=(batch, (n1 * fw) // wc),
        in_specs=[
            pl.BlockSpec((None, 2, n2, wc), lambda b, j: (b, 0, 0, j)),
            pl.BlockSpec((n2, 2 * n2), lambda b, j: (0, 0)),
        ],
        out_specs=pl.BlockSpec((n2, wc), lambda b, j: (b, j)),
        out_shape=jax.ShapeDtypeStruct((rows // n1, n1 * fw), BF16),
        compiler_params=_cparams(("parallel", "parallel")),
        name="dft_rows_b",
    )(z2v, p["dft_b"])
    return y.reshape(rows, fw)


def _softmax_step(carry, k_blk, vt_pieces, qt):
    m, l, acc = carry
    s = _dot(k_blk, qt)
    m_new = jnp.maximum(m, jnp.max(s, axis=0, keepdims=True))
    p = jnp.exp2(s - m_new)
    alpha = jnp.exp2(m - m_new)
    l = alpha * l + jnp.sum(p, axis=0, keepdims=True)
    pb = p.astype(BF16)
    pv = None
    for c, vt in enumerate(vt_pieces):
        part = _dot(vt, pb[c * KEY_CHUNK:(c + 1) * KEY_CHUNK])
        pv = part if pv is None else pv + part
    return m_new, l, alpha * acc + pv


def _attn_kernel(qt_ref, k_ref, kc_ref, vt_ref, vtc_ref, o_ref, *, tk):
    qt = qt_ref[...]
    tq = qt.shape[1]
    per = tk // KEY_CHUNK

    def body(j, carry):
        k_blk = k_ref[pl.ds(pl.multiple_of(j * tk, tk), tk), :]
        pieces = [vt_ref[j * per + c] for c in range(per)]
        return _softmax_step(carry, k_blk, pieces, qt)

    init = (jnp.full((1, tq), -1e30, F32), jnp.zeros((1, tq), F32), jnp.zeros((V_HEAD_DIM, tq), F32))
    carry = lax.fori_loop(0, k_ref.shape[0] // tk, body, init)
    m, l, acc = _softmax_step(carry, kc_ref[...], [vtc_ref[0]], qt)
    o_ref[...] = (acc / l).T.astype(BF16)


def _attention(qt, k, vt, *, batch, seq, ctx_len, heads):
    rows = k.shape[1]
    tq, tk = ATT_TQ, ATT_TK
    nq = seq // tq
    lat_rows = batch * seq
    return pl.pallas_call(
        functools.partial(_attn_kernel, tk=tk),
        grid=(batch, heads, nq),
        in_specs=[
            pl.BlockSpec((None, QK_DIM, tq), lambda b, h, i: (h, 0, b * nq + i)),
            pl.BlockSpec((None, seq, QK_DIM), lambda b, h, i: (h, b, 0)),
            pl.BlockSpec((None, ctx_len, QK_DIM), lambda b, h, i: (h, lat_rows // ctx_len + b, 0)),
            pl.BlockSpec((None, seq // KEY_CHUNK, V_HEAD_DIM, KEY_CHUNK), lambda b, h, i: (h, b, 0, 0)),
            pl.BlockSpec((None, 1, V_HEAD_DIM, KEY_CHUNK), lambda b, h, i: (h, lat_rows // KEY_CHUNK + b, 0, 0)),
        ],
        out_specs=pl.BlockSpec((tq, V_HEAD_DIM), lambda b, h, i: (b * nq + i, h)),
        out_shape=jax.ShapeDtypeStruct((rows, heads * V_HEAD_DIM), BF16),
        compiler_params=_cparams(("parallel", "parallel", "arbitrary")),
        name="attention",
    )(qt, k, k, vt, vt)


def _ctx_kernel(qt_ref, k_ref, vt_ref, zr_ref, zi_ref, fc_ref, o_in, y_in, o_ref, y_ref, *, heads):
    del o_in, y_in
    for hd in range(heads):
        s = _dot(k_ref[hd], qt_ref[hd])
        p = jnp.exp2(s - jnp.max(s, axis=0, keepdims=True))
        l = jnp.sum(p, axis=0, keepdims=True)
        o = _dot(vt_ref[hd, 0], p.astype(BF16)) / l
        o_ref[:, hd * V_HEAD_DIM:(hd + 1) * V_HEAD_DIM] = o.T.astype(BF16)
    zst = jnp.concatenate([zr_ref[...], zi_ref[...]], axis=0)
    y_ref[...] = _dot(fc_ref[...], zst).astype(BF16)


def _ctx_mixer(qt, k, vt, zr, zi, o, y, p, *, batch, lat_rows, ctx_len, heads, fw):
    c0 = lat_rows // ctx_len
    return pl.pallas_call(
        functools.partial(_ctx_kernel, heads=heads),
        grid=(batch,),
        in_specs=[
            pl.BlockSpec((heads, QK_DIM, ctx_len), lambda b: (0, 0, c0 + b)),
            pl.BlockSpec((heads, ctx_len, QK_DIM), lambda b: (0, c0 + b, 0)),
            pl.BlockSpec((heads, 1, V_HEAD_DIM, KEY_CHUNK), lambda b: (0, c0 + b, 0, 0)),
            pl.BlockSpec((ctx_len, fw), lambda b: (c0 + b, 0)),
            pl.BlockSpec((ctx_len, fw), lambda b: (c0 + b, 0)),
            pl.BlockSpec((ctx_len, 2 * ctx_len), lambda b: (0, 0)),
            pl.BlockSpec(memory_space=pl.ANY),
            pl.BlockSpec(memory_space=pl.ANY),
        ],
        out_specs=[
            pl.BlockSpec((ctx_len, heads * V_HEAD_DIM), lambda b: (c0 + b, 0)),
            pl.BlockSpec((ctx_len, fw), lambda b: (c0 + b, 0)),
        ],
        out_shape=[jax.ShapeDtypeStruct(o.shape, o.dtype), jax.ShapeDtypeStruct(y.shape, y.dtype)],
        input_output_aliases={6: 0, 7: 1},
        compiler_params=_cparams(("parallel",)),
        name="ctx_mixer",
    )(qt, k, vt, zr, zi, p["dft_ctx"], o, y)


def _mixer_out_kernel(x_ref, mod_ref, a_ref, y_ref, wf_ref, wo_ref, o_ref, *, groups):
    ch = y_ref.shape[1] // groups
    parts = [a_ref[...]]
    for gi in range(groups):
        parts.append(_dot(y_ref[:, gi * ch:(gi + 1) * ch], wf_ref[gi]).astype(BF16))
    cat = jnp.concatenate(parts, axis=1)
    o_ref[...] = x_ref[...] + mod_ref[5:6, :] * _dot(cat, wo_ref[...])


def _mixer_out(xa, mods, o, y, p, layer, *, rows, seq, groups):
    d = xa.shape[1]
    tm = ROW_TILE
    aw, fw = o.shape[1], y.shape[1]
    ch = fw // groups
    row = lambda i: (i, 0)
    return pl.pallas_call(
        functools.partial(_mixer_out_kernel, groups=groups),
        grid=(rows // tm,),
        in_specs=[
            pl.BlockSpec((tm, d), row),
            pl.BlockSpec((None, N_MOD, d), lambda i: ((i * tm) // seq, 0, 0)),
            pl.BlockSpec((tm, aw), row),
            pl.BlockSpec((tm, fw), row),
            pl.BlockSpec((None, groups, ch, ch), lambda i: (layer, 0, 0, 0), pipeline_mode=pl.Buffered(1)),
            pl.BlockSpec((None, aw + fw, d), lambda i: (layer, 0, 0), pipeline_mode=pl.Buffered(1)),
        ],
        out_specs=pl.BlockSpec((tm, d), row),
        out_shape=jax.ShapeDtypeStruct((rows, d), F32),
        compiler_params=_cparams(("parallel",)),
        name="mixer_out",
    )(xa, mods, o, y, p["w_f"], p["w_o"])


def _dft_parts(n):
    idx = np.arange(n, dtype=np.int64)
    ang = 2.0 * np.pi * ((idx[:, None] * idx[None, :]) % n).astype(np.float64) / n
    s = 1.0 / math.sqrt(n)
    return np.cos(ang) * s, np.sin(ang) * s


def _tables(seq, ctx_len, ch, n1, n2, tm):
    t = {}
    c, s = _dft_parts(ch)
    t["dft_ch"] = jnp.asarray(np.concatenate([c, -s], axis=1), BF16)
    c, s = _dft_parts(n1)
    t["dft_a"] = jnp.asarray(np.block([[c, s], [-s, c]]), BF16)
    c, s = _dft_parts(n2)
    t["dft_b"] = jnp.asarray(np.concatenate([c, s], axis=1), BF16)
    c, s = _dft_parts(ctx_len)
    t["dft_ctx"] = jnp.asarray(np.concatenate([c, s], axis=1), BF16)
    i2 = np.arange(n2, dtype=np.int64)[:, None]
    k1 = np.arange(n1, dtype=np.int64)[None, :]
    ang = 2.0 * np.pi * ((i2 * k1) % seq).astype(np.float64) / seq
    t["tw_c"] = jnp.asarray(np.broadcast_to(np.cos(ang)[:, :, None], (n2, n1, LANES)), F32)
    t["tw_s"] = jnp.asarray(np.broadcast_to(np.sin(ang)[:, :, None], (n2, n1, LANES)), F32)

    rows = seq // GRID_W
    row = jnp.broadcast_to(jnp.arange(rows, dtype=F32)[:, None], (rows, GRID_W)).reshape(-1)
    col = jnp.broadcast_to(jnp.arange(GRID_W, dtype=F32)[None, :], (rows, GRID_W)).reshape(-1)
    inv_freq = ROPE_THETA ** (-jnp.arange(AXIS_FREQS, dtype=F32) / AXIS_FREQS)
    ang_r = row[:, None] * inv_freq
    ang_c = col[:, None] * inv_freq
    cr, sr, cc, sc = jnp.cos(ang_r), jnp.sin(ang_r), jnp.cos(ang_c), jnp.sin(ang_c)
    cos = jnp.concatenate([cr, cr, cc, cc], axis=1)
    sin = jnp.concatenate([-sr, sr, -sc, sc], axis=1)
    cos = jnp.concatenate([cos, jnp.ones((tm, QK_ROPE_DIM), F32)], axis=0)
    sin = jnp.concatenate([sin, jnp.zeros((tm, QK_ROPE_DIM), F32)], axis=0)
    pad = ((0, 0), (0, LANES - QK_ROPE_DIM))
    t["rope_ck"] = jnp.pad(cos, pad)
    t["rope_sk"] = jnp.pad(sin, pad)
    t["rope_cqt"] = cos.T
    t["rope_sqt"] = sin.T
    return t


def _swap_perm():
    a = AXIS_FREQS
    return np.concatenate([np.arange(a, 2 * a), np.arange(0, a), np.arange(3 * a, 4 * a), np.arange(2 * a, 3 * a)])


def kernel(x, c, ctx, c_ctx, w_ada, b_ada, norm_g, ffn1_w_gate, ffn1_w_up, ffn1_w_down, ffn2_w_gate, ffn2_w_up,
           ffn2_w_down, w_in, q_lat_g, w_uq, kv_lat_g, w_ukv, q_norm_g, k_norm_g, w_fourier, w_o):
    batch, seq, d = x.shape
    ctx_len = ctx.shape[1]
    depth = w_ada.shape[0]
    rq = q_lat_g.shape[1]
    rkv = kv_lat_g.shape[1]
    heads = w_uq.shape[2] // QK_DIM
    groups = w_fourier.shape[1]
    ch = w_fourier.shape[2]
    fw = groups * ch
    tm = ROW_TILE
    lat_rows = batch * seq
    ctx_rows = batch * ctx_len
    rows = lat_rows + ctx_rows
    n1 = 1 << ((seq.bit_length() - 1) // 2)
    n2 = seq // n1
    assert n1 * n2 == seq and n1 % 16 == 0 and n2 % 16 == 0
    assert ctx_len == KEY_CHUNK and seq % tm == 0 and ctx_rows % tm == 0 and ctx_rows <= seq
    assert rows % n1 == 0 and rows % n2 == 0 and seq % ATT_TQ == 0 and seq % ATT_TK == 0
    assert batch + 1 <= COND_ROWS and w_in.shape[2] == rq + rkv + QK_ROPE_DIM + fw
    assert w_ukv.shape[2] == heads * (QK_NOPE_DIM + V_HEAD_DIM)

    perm = _swap_perm()
    k0 = rq + rkv
    zpad = jnp.zeros((depth, d, LANES - QK_ROPE_DIM), w_in.dtype)
    w_rope = w_in[:, :, k0:k0 + QK_ROPE_DIM]
    w_in_ext = jnp.concatenate(
        [w_in[:, :, :k0], w_in[:, :, k0 + QK_ROPE_DIM:], w_rope, zpad, w_rope[:, :, perm], zpad], axis=2)
    w_ukv4 = w_ukv.reshape(depth, rkv, heads, QK_NOPE_DIM + V_HEAD_DIM)
    gpad = ((0, 0), (0, 0), (0, LANES - QK_ROPE_DIM))
    g_rope = k_norm_g[:, None, QK_NOPE_DIM:]
    p = _tables(seq, ctx_len, ch, n1, n2, tm)
    p.update(
        w_in=w_in_ext.astype(BF16),
        q_lat_g=q_lat_g[:, None, :],
        w_uqt=jnp.swapaxes(w_uq, 1, 2).astype(BF16),
        gq=q_norm_g[:, :, None],
        kv_lat_g=kv_lat_g[:, None, :],
        w_uk=w_ukv4[..., :QK_NOPE_DIM].reshape(depth, rkv, heads * QK_NOPE_DIM).astype(BF16),
        w_uvt=jnp.swapaxes(w_ukv4[..., QK_NOPE_DIM:].reshape(depth, rkv, heads * V_HEAD_DIM), 1, 2).astype(BF16),
        gk_n=k_norm_g[:, None, :QK_NOPE_DIM],
        gk_r=jnp.pad(g_rope, gpad),
        gk_rs=jnp.pad(g_rope[:, :, perm], gpad),
        w_f=w_fourier.astype(BF16),
        w_o=w_o.astype(BF16),
    )
    ffn1 = tuple(w.astype(BF16) for w in (ffn1_w_gate, ffn1_w_up, ffn1_w_down))
    ffn2 = tuple(w.astype(BF16) for w in (ffn2_w_gate, ffn2_w_up, ffn2_w_down))

    cond = jnp.concatenate([c, c_ctx[None, :], jnp.zeros((COND_ROWS - batch - 1, d), c.dtype)], axis=0)
    mods_all = _adaln(cond, w_ada, b_ada).reshape(depth, COND_ROWS, N_MOD, d)

    xa = jnp.concatenate([x.reshape(lat_rows, d), ctx.reshape(ctx_rows, d)], axis=0)
    dims = (heads, rq, rkv, fw, groups)
    for layer in range(depth):
        last = layer == depth - 1
        mods = mods_all[layer]
        g = norm_g[layer]
        xa = _ffn(xa, mods, g[0:1], *ffn1, layer, mod0=0, rows=rows, seq=seq)
        qt, k, vt, zr, zi = _mixer_in(xa, mods, g[1:2], p, layer, seq=seq, lat_rows=lat_rows, dims=dims)
        z2 = _dft_a(zr, zi, p, batch=batch, n1=n1, n2=n2, fw=fw)
        y = _dft_b(z2, p, rows=rows, batch=batch, n1=n1, n2=n2, fw=fw)
        o = _attention(qt, k, vt, batch=batch, seq=seq, ctx_len=ctx_len, heads=heads)
        if not last:
            o, y = _ctx_mixer(qt, k, vt, zr, zi, o, y, p, batch=batch, lat_rows=lat_rows, ctx_len=ctx_len,
                              heads=heads, fw=fw)
        out_rows = lat_rows if last else rows
        xa = _mixer_out(xa, mods, o, y, p, layer, rows=out_rows, seq=seq, groups=groups)
        xa = _ffn(xa, mods, g[2:3], *ffn2, layer, mod0=6, rows=out_rows, seq=seq)
    return xa.reshape(batch, seq, d)
```

```python
import functools
import math

import jax
import jax.numpy as jnp
import numpy as np
from jax import lax
from jax.experimental import pallas as pl
from jax.experimental.pallas import tpu as pltpu

GRID_W = 64
V_HEAD_DIM = 128
QK_NOPE_DIM = 128
QK_ROPE_DIM = 64
QK_DIM = QK_NOPE_DIM + QK_ROPE_DIM
FOURIER_GROUPS = 4
N_MOD = 9
AXIS_FREQS = QK_ROPE_DIM // 4
ROPE_THETA = 10000.0
SM_SCALE = QK_DIM ** -0.5
RMS_EPS = 1e-6
LOG2E = 1.4426950408889634

LANES = 128
COND_ROWS = 8
VMEM_LIMIT = 56 * 1024 * 1024

ROW_TILE = 512
KEY_CHUNK = 256
FFN_TILE = 512
ADA_TILE = 1024
ATT_TQ = 512
ATT_TK = 512
ATT_UNROLL = 4

BF16 = jnp.bfloat16
F32 = jnp.float32


def _dot(a, b):
    return jnp.dot(a, b, preferred_element_type=F32)


def _dot_nt(a, b):
    return lax.dot_general(a, b, (((1,), (1,)), ((), ())), preferred_element_type=F32)


def _cparams(sem):
    return pltpu.CompilerParams(dimension_semantics=sem, vmem_limit_bytes=VMEM_LIMIT)


def _norm_mod(x, g, shift, scale):
    y = x * lax.rsqrt(jnp.mean(x * x, axis=-1, keepdims=True) + RMS_EPS) * g
    return y * (1.0 + scale) + shift


def _adaln_kernel(c_ref, w_ref, b_ref, o_ref):
    c = c_ref[...]
    s = (c * jax.nn.sigmoid(c)).astype(BF16)
    o_ref[...] = _dot(s, w_ref[...].astype(BF16)) + b_ref[...]


def _adaln(cond, w_ada, b_ada):
    depth, d, n = w_ada.shape
    tn = math.gcd(ADA_TILE, d)
    return pl.pallas_call(
        _adaln_kernel,
        grid=(depth, n // tn),
        in_specs=[
            pl.BlockSpec((COND_ROWS, d), lambda l, j: (0, 0)),
            pl.BlockSpec((None, d, tn), lambda l, j: (l, 0, j)),
            pl.BlockSpec((None, 1, tn), lambda l, j: (l, 0, j)),
        ],
        out_specs=pl.BlockSpec((None, COND_ROWS, tn), lambda l, j: (l, 0, j)),
        out_shape=jax.ShapeDtypeStruct((depth, COND_ROWS, n), F32),
        compiler_params=_cparams(("arbitrary", "arbitrary")),
        name="adaln",
    )(cond, w_ada, b_ada.reshape(depth, 1, n))


def _ffn_kernel(x_ref, mod_ref, g_ref, wg_ref, wu_ref, wd_ref, o_ref, h_ref, acc_ref, *, mod0):
    f = pl.program_id(1)

    @pl.when(f == 0)
    def _():
        h = _norm_mod(x_ref[...], g_ref[...], mod_ref[mod0:mod0 + 1, :], mod_ref[mod0 + 1:mod0 + 2, :])
        h_ref[...] = h.astype(BF16)
        acc_ref[...] = jnp.zeros_like(acc_ref)

    h = h_ref[...]
    gate = _dot(h, wg_ref[...])
    up = _dot(h, wu_ref[...])
    a = (gate * jax.nn.sigmoid(gate) * up).astype(BF16)
    acc_ref[...] += _dot(a, wd_ref[...])

    @pl.when(f == pl.num_programs(1) - 1)
    def _():
        o_ref[...] = x_ref[...] + (0.5 * mod_ref[mod0 + 2:mod0 + 3, :]) * acc_ref[...]


def _ffn(xa, mods, g, wg, wu, wd, layer, *, mod0, rows, seq):
    d = xa.shape[1]
    d_ff = wg.shape[2]
    tm, tf = ROW_TILE, FFN_TILE
    row = lambda i, f: (i, 0)
    return pl.pallas_call(
        functools.partial(_ffn_kernel, mod0=mod0),
        grid=(rows // tm, d_ff // tf),
        in_specs=[
            pl.BlockSpec((tm, d), row),
            pl.BlockSpec((None, N_MOD, d), lambda i, f: ((i * tm) // seq, 0, 0)),
            pl.BlockSpec((1, d), lambda i, f: (0, 0)),
            pl.BlockSpec((None, d, tf), lambda i, f: (layer, 0, f)),
            pl.BlockSpec((None, d, tf), lambda i, f: (layer, 0, f)),
            pl.BlockSpec((None, tf, d), lambda i, f: (layer, f, 0)),
        ],
        out_specs=pl.BlockSpec((tm, d), row),
        out_shape=jax.ShapeDtypeStruct((rows, d), F32),
        scratch_shapes=[pltpu.VMEM((tm, d), BF16), pltpu.VMEM((tm, d), F32)],
        compiler_params=_cparams(("parallel", "arbitrary")),
        name="ffn",
    )(xa, mods, g, wg, wu, wd)


def _mixer_in_kernel(x_ref, mod_ref, g_ref, win_ref, gql_ref, wuqt_ref, gq_ref, gkvl_ref, wuk_ref, wuvt_ref,
                     gkn_ref, gkr_ref, gkrs_ref, ck_ref, sk_ref, cq_ref, sq_ref, dft_ref,
                     qt_ref, k_ref, vt_ref, zr_ref, zi_ref, *, heads, rq, rkv, fw, groups):
    tm = x_ref.shape[0]
    h = _norm_mod(x_ref[...], g_ref[...], mod_ref[3:4, :], mod_ref[4:5, :]).astype(BF16)
    proj = _dot(h, win_ref[...])
    f0 = rq + rkv
    r0 = f0 + fw

    qlat = proj[:, :rq]
    qn = (qlat * lax.rsqrt(jnp.mean(qlat * qlat, axis=-1, keepdims=True) + RMS_EPS) * gql_ref[...]).astype(BF16)
    qt = _dot_nt(wuqt_ref[...], qn)
    gq = gq_ref[...]
    cq = cq_ref[...]
    sq = sq_ref[...]
    a = AXIS_FREQS
    for hd in range(heads):
        blk = qt[hd * QK_DIM:(hd + 1) * QK_DIM]
        r = lax.rsqrt(jnp.mean(blk * blk, axis=0, keepdims=True) + RMS_EPS) * (SM_SCALE * LOG2E)
        blk = blk * r * gq
        rp = blk[QK_NOPE_DIM:]
        sw = jnp.concatenate([rp[a:2 * a], rp[:a], rp[3 * a:], rp[2 * a:3 * a]], axis=0)
        qt_ref[hd, :QK_NOPE_DIM, :] = blk[:QK_NOPE_DIM].astype(BF16)
        qt_ref[hd, QK_NOPE_DIM:, :] = (rp * cq + sw * sq).astype(BF16)

    kvlat = proj[:, rq:f0]
    kvn = (kvlat * lax.rsqrt(jnp.mean(kvlat * kvlat, axis=-1, keepdims=True) + RMS_EPS) * gkvl_ref[...]).astype(BF16)
    knope = _dot(kvn, wuk_ref[...])
    vt = _dot_nt(wuvt_ref[...], kvn)
    for c in range(tm // KEY_CHUNK):
        piece = vt[:, c * KEY_CHUNK:(c + 1) * KEY_CHUNK].astype(BF16)
        vt_ref[:, c] = piece.reshape(heads, V_HEAD_DIM, KEY_CHUNK)
    kr = proj[:, r0:r0 + LANES]
    krs = proj[:, r0 + LANES:r0 + 2 * LANES]
    krr = kr * gkr_ref[...] * ck_ref[...] + krs * gkrs_ref[...] * sk_ref[...]
    ss_r = jnp.sum(kr * kr, axis=-1, keepdims=True)
    gkn = gkn_ref[...]
    for hd in range(heads):
        kn = knope[:, hd * QK_NOPE_DIM:(hd + 1) * QK_NOPE_DIM]
        ss = jnp.sum(kn * kn, axis=-1, keepdims=True) + ss_r
        r = lax.rsqrt(ss * (1.0 / QK_DIM) + RMS_EPS)
        k_ref[hd, :, :QK_NOPE_DIM] = (kn * r * gkn).astype(BF16)
        k_ref[hd, :, QK_NOPE_DIM:] = (krr * r)[:, :QK_ROPE_DIM].astype(BF16)

    ch = fw // groups
    fb = proj[:, f0:r0].astype(BF16)
    dft = dft_ref[...]
    for gi in range(groups):
        uw = _dot(fb[:, gi * ch:(gi + 1) * ch], dft)
        zr_ref[:, gi * ch:(gi + 1) * ch] = uw[:, :ch].astype(BF16)
        zi_ref[:, gi * ch:(gi + 1) * ch] = uw[:, ch:].astype(BF16)


def _mixer_in(xa, mods, g, p, layer, *, seq, lat_rows, dims):
    rows, d = xa.shape
    heads, rq, rkv, fw, groups = dims
    tm = ROW_TILE
    ncols = p["w_in"].shape[2]
    n_lat = lat_rows // tm
    per_seq = seq // tm
    const2 = lambda i: (0, 0)
    row = lambda i: (i, 0)
    pos = lambda i: (jnp.where(i < n_lat, i % per_seq, per_seq), 0)
    post = lambda i: (0, jnp.where(i < n_lat, i % per_seq, per_seq))
    lay = lambda *shape: pl.BlockSpec((None,) + shape, lambda i: (layer,) + (0,) * len(shape),
                                      pipeline_mode=pl.Buffered(1))
    ch = fw // groups
    out_shape = [
        jax.ShapeDtypeStruct((heads, QK_DIM, rows), BF16),
        jax.ShapeDtypeStruct((heads, rows, QK_DIM), BF16),
        jax.ShapeDtypeStruct((heads, rows // KEY_CHUNK, V_HEAD_DIM, KEY_CHUNK), BF16),
        jax.ShapeDtypeStruct((rows, fw), BF16),
        jax.ShapeDtypeStruct((rows, fw), BF16),
    ]
    out_specs = [
        pl.BlockSpec((heads, QK_DIM, tm), lambda i: (0, 0, i)),
        pl.BlockSpec((heads, tm, QK_DIM), lambda i: (0, i, 0)),
        pl.BlockSpec((heads, tm // KEY_CHUNK, V_HEAD_DIM, KEY_CHUNK), lambda i: (0, i, 0, 0)),
        pl.BlockSpec((tm, fw), row),
        pl.BlockSpec((tm, fw), row),
    ]
    in_specs = [
        pl.BlockSpec((tm, d), row),
        pl.BlockSpec((None, N_MOD, d), lambda i: ((i * tm) // seq, 0, 0)),
        pl.BlockSpec((1, d), const2),
        lay(d, ncols),
        lay(1, rq),
        lay(heads * QK_DIM, rq),
        lay(QK_DIM, 1),
        lay(1, rkv),
        lay(rkv, heads * QK_NOPE_DIM),
        lay(heads * V_HEAD_DIM, rkv),
        lay(1, LANES), lay(1, LANES), lay(1, LANES),
        pl.BlockSpec((tm, LANES), pos), pl.BlockSpec((tm, LANES), pos),
        pl.BlockSpec((QK_ROPE_DIM, tm), post), pl.BlockSpec((QK_ROPE_DIM, tm), post),
        pl.BlockSpec((ch, 2 * ch), const2),
    ]
    return pl.pallas_call(
        functools.partial(_mixer_in_kernel, heads=heads, rq=rq, rkv=rkv, fw=fw, groups=groups),
        grid=(rows // tm,),
        in_specs=in_specs,
        out_specs=out_specs,
        out_shape=out_shape,
        compiler_params=_cparams(("parallel",)),
        name="mixer_in",
    )(xa, mods, g, p["w_in"], p["q_lat_g"], p["w_uqt"], p["gq"], p["kv_lat_g"], p["w_uk"], p["w_uvt"],
      p["gk_n"], p["gk_r"], p["gk_rs"], p["rope_ck"], p["rope_sk"], p["rope_cqt"], p["rope_sqt"], p["dft_ch"])


def _dft_a_kernel(zr_ref, zi_ref, fa_ref, tc_ref, ts_ref, o_ref, *, fw):
    n1 = zr_ref.shape[0]
    fa = fa_ref[...]
    reps = fw // LANES
    for t in range(zr_ref.shape[1] // fw):
        cols = slice(t * fw, (t + 1) * fw)
        zst = jnp.concatenate([zr_ref[:, cols], zi_ref[:, cols]], axis=0)
        z1 = _dot(fa, zst)
        z1r, z1i = z1[:n1], z1[n1:]
        tc = jnp.concatenate([tc_ref[t]] * reps, axis=1)
        ts = jnp.concatenate([ts_ref[t]] * reps, axis=1)
        o_ref[0, t] = (z1r * tc + z1i * ts).astype(BF16)
        o_ref[1, t] = (z1i * tc - z1r * ts).astype(BF16)


def _dft_a(zr, zi, p, *, batch, n1, n2, fw):
    rows = zr.shape[0]
    cb = 8
    zr2 = zr.reshape(rows // n2, n2 * fw)
    zi2 = zi.reshape(rows // n2, n2 * fw)
    blk = pl.BlockSpec((n1, cb * fw), lambda b, j: (b, j))
    tw = pl.BlockSpec((cb, n1, LANES), lambda b, j: (j, 0, 0))
    return pl.pallas_call(
        functools.partial(_dft_a_kernel, fw=fw),
        grid=(batch, n2 // cb),
        in_specs=[blk, blk, pl.BlockSpec((2 * n1, 2 * n1), lambda b, j: (0, 0)), tw, tw],
        out_specs=pl.BlockSpec((None, 2, cb, n1, fw), lambda b, j: (b, 0, j, 0, 0)),
        out_shape=jax.ShapeDtypeStruct((batch, 2, n2, n1, fw), BF16),
        compiler_params=_cparams(("parallel", "parallel")),
        name="dft_rows_a",
    )(zr2, zi2, p["dft_a"], p["tw_c"], p["tw_s"])


def _dft_b_kernel(z_ref, fb_ref, o_ref):
    n2 = z_ref.shape[1]
    zst = z_ref[...].reshape(2 * n2, z_ref.shape[2])
    o_ref[...] = _dot(fb_ref[...], zst).astype(BF16)


def _dft_b(z2, p, *, rows, batch, n1, n2, fw):
    wc = 8 * fw
    z2v = z2.reshape(batch, 2, n2, n1 * fw)
    y = pl.pallas_call(
        _dft_b_kernel,
        grid=(batch, (n1 * fw) // wc),
        in_specs=[
            pl.BlockSpec((None, 2, n2, wc), lambda b, j: (b, 0, 0, j)),
            pl.BlockSpec((n2, 2 * n2), lambda b, j: (0, 0)),
        ],
        out_specs=pl.BlockSpec((n2, wc), lambda b, j: (b, j)),
        out_shape=jax.ShapeDtypeStruct((rows // n1, n1 * fw), BF16),
        compiler_params=_cparams(("parallel", "parallel")),
        name="dft_rows_b",
    )(z2v, p["dft_b"])
    return y.reshape(rows, fw)


def _attn_kernel(qt_ref, k_ref, kc_ref, vt_ref, vtc_ref, o_ref, s_ref, acc_ref, *, tk, unroll):
    qt = qt_ref[...]
    per = tk // KEY_CHUNK
    n = k_ref.shape[0] // tk

    def scores(j, slot):
        s = _dot(k_ref[pl.ds(pl.multiple_of(j * tk, tk), tk), :], qt)
        s_ref[slot] = s
        return jnp.max(s, axis=0, keepdims=True)

    def absorb(s, vts, cmax, m, l):
        m_new = jnp.maximum(m, cmax)
        p = jnp.exp2(s - m_new)
        alpha = jnp.exp2(m - m_new)
        l = alpha * l + jnp.sum(p, axis=0, keepdims=True)
        pb = p.astype(BF16)
        pv = _dot(vts[0], pb[:KEY_CHUNK])
        for c in range(1, len(vts)):
            pv = pv + _dot(vts[c], pb[c * KEY_CHUNK:(c + 1) * KEY_CHUNK])
        acc_ref[...] = alpha * acc_ref[...] + pv
        return m_new, l

    def absorb_chunk(j, slot, cmax, m, l):
        return absorb(s_ref[slot], [vt_ref[j * per + c] for c in range(per)], cmax, m, l)

    tq = qt.shape[1]
    acc_ref[...] = jnp.zeros_like(acc_ref)
    sc = _dot(kc_ref[...], qt)
    cmax0 = scores(0, 0)
    m, l = absorb(sc, [vtc_ref[0]], jnp.max(sc, axis=0, keepdims=True),
                  jnp.full((1, tq), -1e30, F32), jnp.zeros((1, tq), F32))

    def group(i, carry):
        m, l, cmax = carry
        for u in range(unroll):
            j = i * unroll + u
            cmax_next = scores(jnp.minimum(j + 1, n - 1), (u + 1) % 2)
            m, l = absorb_chunk(j, u % 2, cmax, m, l)
            cmax = cmax_next
        return m, l, cmax

    m, l, _ = lax.fori_loop(0, n // unroll, group, (m, l, cmax0))
    o_ref[...] = (acc_ref[...] / l).T.astype(BF16)


def _attention(qt, k, vt, *, batch, seq, ctx_len, heads):
    rows = k.shape[1]
    tq, tk = ATT_TQ, ATT_TK
    nq = seq // tq
    lat_rows = batch * seq
    return pl.pallas_call(
        functools.partial(_attn_kernel, tk=tk, unroll=math.gcd(ATT_UNROLL, seq // tk)),
        grid=(batch, heads, nq),
        in_specs=[
            pl.BlockSpec((None, QK_DIM, tq), lambda b, h, i: (h, 0, b * nq + i)),
            pl.BlockSpec((None, seq, QK_DIM), lambda b, h, i: (h, b, 0)),
            pl.BlockSpec((None, ctx_len, QK_DIM), lambda b, h, i: (h, lat_rows // ctx_len + b, 0)),
            pl.BlockSpec((None, seq // KEY_CHUNK, V_HEAD_DIM, KEY_CHUNK), lambda b, h, i: (h, b, 0, 0)),
            pl.BlockSpec((None, 1, V_HEAD_DIM, KEY_CHUNK), lambda b, h, i: (h, lat_rows // KEY_CHUNK + b, 0, 0)),
        ],
        out_specs=pl.BlockSpec((tq, V_HEAD_DIM), lambda b, h, i: (b * nq + i, h)),
        out_shape=jax.ShapeDtypeStruct((rows, heads * V_HEAD_DIM), BF16),
        scratch_shapes=[pltpu.VMEM((2, tk, tq), F32), pltpu.VMEM((V_HEAD_DIM, tq), F32)],
        compiler_params=_cparams(("parallel", "parallel", "arbitrary")),
        name="attention",
    )(qt, k, k, vt, vt)


def _ctx_kernel(qt_ref, k_ref, vt_ref, zr_ref, zi_ref, fc_ref, o_in, y_in, o_ref, y_ref, *, heads):
    del o_in, y_in
    for hd in range(heads):
        s = _dot(k_ref[hd], qt_ref[hd])
        p = jnp.exp2(s - jnp.max(s, axis=0, keepdims=True))
        l = jnp.sum(p, axis=0, keepdims=True)
        o = _dot(vt_ref[hd, 0], p.astype(BF16)) / l
        o_ref[:, hd * V_HEAD_DIM:(hd + 1) * V_HEAD_DIM] = o.T.astype(BF16)
    zst = jnp.concatenate([zr_ref[...], zi_ref[...]], axis=0)
    y_ref[...] = _dot(fc_ref[...], zst).astype(BF16)


def _ctx_mixer(qt, k, vt, zr, zi, o, y, p, *, batch, lat_rows, ctx_len, heads, fw):
    c0 = lat_rows // ctx_len
    return pl.pallas_call(
        functools.partial(_ctx_kernel, heads=heads),
        grid=(batch,),
        in_specs=[
            pl.BlockSpec((heads, QK_DIM, ctx_len), lambda b: (0, 0, c0 + b)),
            pl.BlockSpec((heads, ctx_len, QK_DIM), lambda b: (0, c0 + b, 0)),
            pl.BlockSpec((heads, 1, V_HEAD_DIM, KEY_CHUNK), lambda b: (0, c0 + b, 0, 0)),
            pl.BlockSpec((ctx_len, fw), lambda b: (c0 + b, 0)),
            pl.BlockSpec((ctx_len, fw), lambda b: (c0 + b, 0)),
            pl.BlockSpec((ctx_len, 2 * ctx_len), lambda b: (0, 0)),
            pl.BlockSpec(memory_space=pl.ANY),
            pl.BlockSpec(memory_space=pl.ANY),
        ],
        out_specs=[
            pl.BlockSpec((ctx_len, heads * V_HEAD_DIM), lambda b: (c0 + b, 0)),
            pl.BlockSpec((ctx_len, fw), lambda b: (c0 + b, 0)),
        ],
        out_shape=[jax.ShapeDtypeStruct(o.shape, o.dtype), jax.ShapeDtypeStruct(y.shape, y.dtype)],
        input_output_aliases={6: 0, 7: 1},
        compiler_params=_cparams(("parallel",)),
        name="ctx_mixer",
    )(qt, k, vt, zr, zi, p["dft_ctx"], o, y)


def _mixer_out_kernel(x_ref, mod_ref, a_ref, y_ref, wf_ref, wo_ref, o_ref, *, groups):
    ch = y_ref.shape[1] // groups
    parts = [a_ref[...]]
    for gi in range(groups):
        parts.append(_dot(y_ref[:, gi * ch:(gi + 1) * ch], wf_ref[gi]).astype(BF16))
    cat = jnp.concatenate(parts, axis=1)
    o_ref[...] = x_ref[...] + mod_ref[5:6, :] * _dot(cat, wo_ref[...])


def _mixer_out(xa, mods, o, y, p, layer, *, rows, seq, groups):
    d = xa.shape[1]
    tm = ROW_TILE
    aw, fw = o.shape[1], y.shape[1]
    ch = fw // groups
    row = lambda i: (i, 0)
    return pl.pallas_call(
        functools.partial(_mixer_out_kernel, groups=groups),
        grid=(rows // tm,),
        in_specs=[
            pl.BlockSpec((tm, d), row),
            pl.BlockSpec((None, N_MOD, d), lambda i: ((i * tm) // seq, 0, 0)),
            pl.BlockSpec((tm, aw), row),
            pl.BlockSpec((tm, fw), row),
            pl.BlockSpec((None, groups, ch, ch), lambda i: (layer, 0, 0, 0), pipeline_mode=pl.Buffered(1)),
            pl.BlockSpec((None, aw + fw, d), lambda i: (layer, 0, 0), pipeline_mode=pl.Buffered(1)),
        ],
        out_specs=pl.BlockSpec((tm, d), row),
        out_shape=jax.ShapeDtypeStruct((rows, d), F32),
        compiler_params=_cparams(("parallel",)),
        name="mixer_out",
    )(xa, mods, o, y, p["w_f"], p["w_o"])


def _dft_parts(n):
    idx = np.arange(n, dtype=np.int64)
    ang = 2.0 * np.pi * ((idx[:, None] * idx[None, :]) % n).astype(np.float64) / n
    s = 1.0 / math.sqrt(n)
    return np.cos(ang) * s, np.sin(ang) * s


def _tables(seq, ctx_len, ch, n1, n2, tm):
    t = {}
    c, s = _dft_parts(ch)
    t["dft_ch"] = jnp.asarray(np.concatenate([c, -s], axis=1), BF16)
    c, s = _dft_parts(n1)
    t["dft_a"] = jnp.asarray(np.block([[c, s], [-s, c]]), BF16)
    c, s = _dft_parts(n2)
    t["dft_b"] = jnp.asarray(np.concatenate([c, s], axis=1), BF16)
    c, s = _dft_parts(ctx_len)
    t["dft_ctx"] = jnp.asarray(np.concatenate([c, s], axis=1), BF16)
    i2 = np.arange(n2, dtype=np.int64)[:, None]
    k1 = np.arange(n1, dtype=np.int64)[None, :]
    ang = 2.0 * np.pi * ((i2 * k1) % seq).astype(np.float64) / seq
    t["tw_c"] = jnp.asarray(np.broadcast_to(np.cos(ang)[:, :, None], (n2, n1, LANES)), F32)
    t["tw_s"] = jnp.asarray(np.broadcast_to(np.sin(ang)[:, :, None], (n2, n1, LANES)), F32)

    rows = seq // GRID_W
    row = jnp.broadcast_to(jnp.arange(rows, dtype=F32)[:, None], (rows, GRID_W)).reshape(-1)
    col = jnp.broadcast_to(jnp.arange(GRID_W, dtype=F32)[None, :], (rows, GRID_W)).reshape(-1)
    inv_freq = ROPE_THETA ** (-jnp.arange(AXIS_FREQS, dtype=F32) / AXIS_FREQS)
    ang_r = row[:, None] * inv_freq
    ang_c = col[:, None] * inv_freq
    cr, sr, cc, sc = jnp.cos(ang_r), jnp.sin(ang_r), jnp.cos(ang_c), jnp.sin(ang_c)
    cos = jnp.concatenate([cr, cr, cc, cc], axis=1)
    sin = jnp.concatenate([-sr, sr, -sc, sc], axis=1)
    cos = jnp.concatenate([cos, jnp.ones((tm, QK_ROPE_DIM), F32)], axis=0)
    sin = jnp.concatenate([sin, jnp.zeros((tm, QK_ROPE_DIM), F32)], axis=0)
    pad = ((0, 0), (0, LANES - QK_ROPE_DIM))
    t["rope_ck"] = jnp.pad(cos, pad)
    t["rope_sk"] = jnp.pad(sin, pad)
    t["rope_cqt"] = cos.T
    t["rope_sqt"] = sin.T
    return t


def _swap_perm():
    a = AXIS_FREQS
    return np.concatenate([np.arange(a, 2 * a), np.arange(0, a), np.arange(3 * a, 4 * a), np.arange(2 * a, 3 * a)])


def kernel(x, c, ctx, c_ctx, w_ada, b_ada, norm_g, ffn1_w_gate, ffn1_w_up, ffn1_w_down, ffn2_w_gate, ffn2_w_up,
           ffn2_w_down, w_in, q_lat_g, w_uq, kv_lat_g, w_ukv, q_norm_g, k_norm_g, w_fourier, w_o):
    batch, seq, d = x.shape
    ctx_len = ctx.shape[1]
    depth = w_ada.shape[0]
    rq = q_lat_g.shape[1]
    rkv = kv_lat_g.shape[1]
    heads = w_uq.shape[2] // QK_DIM
    groups = w_fourier.shape[1]
    ch = w_fourier.shape[2]
    fw = groups * ch
    tm = ROW_TILE
    lat_rows = batch * seq
    ctx_rows = batch * ctx_len
    rows = lat_rows + ctx_rows
    n1 = 1 << ((seq.bit_length() - 1) // 2)
    n2 = seq // n1
    assert n1 * n2 == seq and n1 % 16 == 0 and n2 % 16 == 0
    assert ctx_len == KEY_CHUNK and seq % tm == 0 and ctx_rows % tm == 0 and ctx_rows <= seq
    assert rows % n1 == 0 and rows % n2 == 0 and seq % ATT_TQ == 0 and seq % (2 * ATT_TK) == 0
    assert batch + 1 <= COND_ROWS and w_in.shape[2] == rq + rkv + QK_ROPE_DIM + fw
    assert w_ukv.shape[2] == heads * (QK_NOPE_DIM + V_HEAD_DIM)

    perm = _swap_perm()
    k0 = rq + rkv
    zpad = jnp.zeros((depth, d, LANES - QK_ROPE_DIM), w_in.dtype)
    w_rope = w_in[:, :, k0:k0 + QK_ROPE_DIM]
    w_in_ext = jnp.concatenate(
        [w_in[:, :, :k0], w_in[:, :, k0 + QK_ROPE_DIM:], w_rope, zpad, w_rope[:, :, perm], zpad], axis=2)
    w_ukv4 = w_ukv.reshape(depth, rkv, heads, QK_NOPE_DIM + V_HEAD_DIM)
    gpad = ((0, 0), (0, 0), (0, LANES - QK_ROPE_DIM))
    g_rope = k_norm_g[:, None, QK_NOPE_DIM:]
    p = _tables(seq, ctx_len, ch, n1, n2, tm)
    p.update(
        w_in=w_in_ext.astype(BF16),
        q_lat_g=q_lat_g[:, None, :],
        w_uqt=jnp.swapaxes(w_uq, 1, 2).astype(BF16),
        gq=q_norm_g[:, :, None],
        kv_lat_g=kv_lat_g[:, None, :],
        w_uk=w_ukv4[..., :QK_NOPE_DIM].reshape(depth, rkv, heads * QK_NOPE_DIM).astype(BF16),
        w_uvt=jnp.swapaxes(w_ukv4[..., QK_NOPE_DIM:].reshape(depth, rkv, heads * V_HEAD_DIM), 1, 2).astype(BF16),
        gk_n=k_norm_g[:, None, :QK_NOPE_DIM],
        gk_r=jnp.pad(g_rope, gpad),
        gk_rs=jnp.pad(g_rope[:, :, perm], gpad),
        w_f=w_fourier.astype(BF16),
        w_o=w_o.astype(BF16),
    )
    ffn1 = tuple(w.astype(BF16) for w in (ffn1_w_gate, ffn1_w_up, ffn1_w_down))
    ffn2 = tuple(w.astype(BF16) for w in (ffn2_w_gate, ffn2_w_up, ffn2_w_down))

    cond = jnp.concatenate([c, c_ctx[None, :], jnp.zeros((COND_ROWS - batch - 1, d), c.dtype)], axis=0)
    mods_all = _adaln(cond, w_ada, b_ada).reshape(depth, COND_ROWS, N_MOD, d)

    xa = jnp.concatenate([x.reshape(lat_rows, d), ctx.reshape(ctx_rows, d)], axis=0)
    dims = (heads, rq, rkv, fw, groups)
    for layer in range(depth):
        last = layer == depth - 1
        mods = mods_all[layer]
        g = norm_g[layer]
        xa = _ffn(xa, mods, g[0:1], *ffn1, layer, mod0=0, rows=rows, seq=seq)
        qt, k, vt, zr, zi = _mixer_in(xa, mods, g[1:2], p, layer, seq=seq, lat_rows=lat_rows, dims=dims)
        z2 = _dft_a(zr, zi, p, batch=batch, n1=n1, n2=n2, fw=fw)
        y = _dft_b(z2, p, rows=rows, batch=batch, n1=n1, n2=n2, fw=fw)
        o = _attention(qt, k, vt, batch=batch, seq=seq, ctx_len=ctx_len, heads=heads)
        if not last:
            o, y = _ctx_mixer(qt, k, vt, zr, zi, o, y, p, batch=batch, lat_rows=lat_rows, ctx_len=ctx_len,
                              heads=heads, fw=fw)
        out_rows = lat_rows if last else rows
        xa = _mixer_out(xa, mods, o, y, p, layer, rows=out_rows, seq=seq, groups=groups)
        xa = _ffn(xa, mods, g[2:3], *ffn2, layer, mod0=6, rows=out_rows, seq=seq)
    return xa.reshape(batch, seq, d)
```

```python
import functools
import math

import jax
import jax.numpy as jnp
import numpy as np
from jax import lax
from jax.experimental import pallas as pl
from jax.experimental.pallas import tpu as pltpu

GRID_W = 64
V_HEAD_DIM = 128
QK_NOPE_DIM = 128
QK_ROPE_DIM = 64
QK_DIM = QK_NOPE_DIM + QK_ROPE_DIM
FOURIER_GROUPS = 4
N_MOD = 9
AXIS_FREQS = QK_ROPE_DIM // 4
ROPE_THETA = 10000.0
SM_SCALE = QK_DIM ** -0.5
RMS_EPS = 1e-6
LOG2E = 1.4426950408889634

LANES = 128
COND_ROWS = 8
VMEM_LIMIT = 56 * 1024 * 1024

ROW_TILE = 512
KEY_CHUNK = 256
FFN_TILE = 512
FFN_PREP_STEPS = 8
ADA_TILE = 1024
ATT_TQ = 512
ATT_TK = 512
ATT_UNROLL = 8

BF16 = jnp.bfloat16
F32 = jnp.float32


def _dot(a, b):
    return jnp.dot(a, b, preferred_element_type=F32)


def _dot_nt(a, b):
    return lax.dot_general(a, b, (((1,), (1,)), ((), ())), preferred_element_type=F32)


def _cparams(sem):
    return pltpu.CompilerParams(dimension_semantics=sem, vmem_limit_bytes=VMEM_LIMIT)


def _norm_mod(x, g, shift, scale):
    y = x * lax.rsqrt(jnp.mean(x * x, axis=-1, keepdims=True) + RMS_EPS) * g
    return y * (1.0 + scale) + shift


def _adaln_kernel(c_ref, w_ref, b_ref, o_ref):
    c = c_ref[...]
    s = (c * jax.nn.sigmoid(c)).astype(BF16)
    o_ref[...] = _dot(s, w_ref[...].astype(BF16)) + b_ref[...]


def _adaln(cond, w_ada, b_ada):
    depth, d, n = w_ada.shape
    tn = math.gcd(ADA_TILE, d)
    return pl.pallas_call(
        _adaln_kernel,
        grid=(depth, n // tn),
        in_specs=[
            pl.BlockSpec((COND_ROWS, d), lambda l, j: (0, 0)),
            pl.BlockSpec((None, d, tn), lambda l, j: (l, 0, j)),
            pl.BlockSpec((None, 1, tn), lambda l, j: (l, 0, j)),
        ],
        out_specs=pl.BlockSpec((None, COND_ROWS, tn), lambda l, j: (l, 0, j)),
        out_shape=jax.ShapeDtypeStruct((depth, COND_ROWS, n), F32),
        compiler_params=_cparams(("arbitrary", "arbitrary")),
        name="adaln",
    )(cond, w_ada, b_ada.reshape(depth, 1, n))


def _ffn_kernel(x_ref, xn_ref, mod_ref, modn_ref, g_ref, wg_ref, wu_ref, wd_ref, o_ref, h_ref, acc_ref, *, mod0):
    i, f = pl.program_id(0), pl.program_id(1)
    slot = i % 2
    tm = x_ref.shape[0]
    g = g_ref[...]

    @pl.when((i == 0) & (f == 0))
    def _():
        h_ref[0] = _norm_mod(x_ref[...], g, mod_ref[mod0:mod0 + 1, :], mod_ref[mod0 + 1:mod0 + 2, :]).astype(BF16)

    h = h_ref[slot]
    gate = _dot(h, wg_ref[...])
    up = _dot(h, wu_ref[...])
    a = (gate * jax.nn.sigmoid(gate) * up).astype(BF16)
    acc_ref[...] = jnp.where(f == 0, 0.0, acc_ref[...]) + _dot(a, wd_ref[...])

    part = tm // FFN_PREP_STEPS
    r0 = pl.multiple_of(jnp.minimum(f, FFN_PREP_STEPS - 1) * part, part)
    hn = _norm_mod(xn_ref[pl.ds(r0, part), :], g, modn_ref[mod0:mod0 + 1, :], modn_ref[mod0 + 1:mod0 + 2, :])
    h_ref[1 - slot, pl.ds(r0, part), :] = hn.astype(BF16)

    @pl.when(f == pl.num_programs(1) - 1)
    def _():
        o_ref[...] = x_ref[...] + (0.5 * mod_ref[mod0 + 2:mod0 + 3, :]) * acc_ref[...]


def _ffn(xa, mods, g, wg, wu, wd, layer, *, mod0, rows, seq):
    d = xa.shape[1]
    d_ff = wg.shape[2]
    tm, tf = ROW_TILE, FFN_TILE
    nt = rows // tm
    assert d_ff // tf >= FFN_PREP_STEPS and tm % (16 * FFN_PREP_STEPS) == 0
    row = lambda i, f: (i, 0)
    nxt = lambda i: jnp.minimum(i + 1, nt - 1)
    return pl.pallas_call(
        functools.partial(_ffn_kernel, mod0=mod0),
        grid=(nt, d_ff // tf),
        in_specs=[
            pl.BlockSpec((tm, d), row),
            pl.BlockSpec((tm, d), lambda i, f: (nxt(i), 0)),
            pl.BlockSpec((None, N_MOD, d), lambda i, f: ((i * tm) // seq, 0, 0)),
            pl.BlockSpec((None, N_MOD, d), lambda i, f: ((nxt(i) * tm) // seq, 0, 0)),
            pl.BlockSpec((1, d), lambda i, f: (0, 0)),
            pl.BlockSpec((None, d, tf), lambda i, f: (layer, 0, f)),
            pl.BlockSpec((None, d, tf), lambda i, f: (layer, 0, f)),
            pl.BlockSpec((None, tf, d), lambda i, f: (layer, f, 0)),
        ],
        out_specs=pl.BlockSpec((tm, d), row),
        out_shape=jax.ShapeDtypeStruct((rows, d), F32),
        scratch_shapes=[pltpu.VMEM((2, tm, d), BF16), pltpu.VMEM((tm, d), F32)],
        compiler_params=_cparams(("arbitrary", "arbitrary")),
        name="ffn",
    )(xa, xa, mods, mods, g, wg, wu, wd)


def _mixer_in_kernel(x_ref, mod_ref, g_ref, win_ref, gql_ref, wuqt_ref, gq_ref, gkvl_ref, wuk_ref, wuvt_ref,
                     gkn_ref, gkr_ref, gkrs_ref, ck_ref, sk_ref, cq_ref, sq_ref, dft_ref,
                     qt_ref, k_ref, vt_ref, zr_ref, zi_ref, *, heads, rq, rkv, fw, groups):
    tm = x_ref.shape[0]
    h = _norm_mod(x_ref[...], g_ref[...], mod_ref[3:4, :], mod_ref[4:5, :]).astype(BF16)
    proj = _dot(h, win_ref[...])
    f0 = rq + rkv
    r0 = f0 + fw

    qlat = proj[:, :rq]
    qn = (qlat * lax.rsqrt(jnp.mean(qlat * qlat, axis=-1, keepdims=True) + RMS_EPS) * gql_ref[...]).astype(BF16)
    qt = _dot_nt(wuqt_ref[...], qn)
    gq = gq_ref[...]
    cq = cq_ref[...]
    sq = sq_ref[...]
    a = AXIS_FREQS
    for hd in range(heads):
        blk = qt[hd * QK_DIM:(hd + 1) * QK_DIM]
        r = lax.rsqrt(jnp.mean(blk * blk, axis=0, keepdims=True) + RMS_EPS) * (SM_SCALE * LOG2E)
        blk = blk * r * gq
        rp = blk[QK_NOPE_DIM:]
        sw = jnp.concatenate([rp[a:2 * a], rp[:a], rp[3 * a:], rp[2 * a:3 * a]], axis=0)
        qt_ref[hd, :QK_NOPE_DIM, :] = blk[:QK_NOPE_DIM].astype(BF16)
        qt_ref[hd, QK_NOPE_DIM:, :] = (rp * cq + sw * sq).astype(BF16)

    kvlat = proj[:, rq:f0]
    kvn = (kvlat * lax.rsqrt(jnp.mean(kvlat * kvlat, axis=-1, keepdims=True) + RMS_EPS) * gkvl_ref[...]).astype(BF16)
    knope = _dot(kvn, wuk_ref[...])
    vt = _dot_nt(wuvt_ref[...], kvn)
    for c in range(tm // KEY_CHUNK):
        piece = vt[:, c * KEY_CHUNK:(c + 1) * KEY_CHUNK].astype(BF16)
        vt_ref[:, c] = piece.reshape(heads, V_HEAD_DIM, KEY_CHUNK)
    kr = proj[:, r0:r0 + LANES]
    krs = proj[:, r0 + LANES:r0 + 2 * LANES]
    krr = kr * gkr_ref[...] * ck_ref[...] + krs * gkrs_ref[...] * sk_ref[...]
    ss_r = jnp.sum(kr * kr, axis=-1, keepdims=True)
    gkn = gkn_ref[...]
    for hd in range(heads):
        kn = knope[:, hd * QK_NOPE_DIM:(hd + 1) * QK_NOPE_DIM]
        ss = jnp.sum(kn * kn, axis=-1, keepdims=True) + ss_r
        r = lax.rsqrt(ss * (1.0 / QK_DIM) + RMS_EPS)
        k_ref[hd, :, :QK_NOPE_DIM] = (kn * r * gkn).astype(BF16)
        k_ref[hd, :, QK_NOPE_DIM:] = (krr * r)[:, :QK_ROPE_DIM].astype(BF16)

    ch = fw // groups
    fb = proj[:, f0:r0].astype(BF16)
    dft = dft_ref[...]
    for gi in range(groups):
        uw = _dot(fb[:, gi * ch:(gi + 1) * ch], dft)
        zr_ref[:, gi * ch:(gi + 1) * ch] = uw[:, :ch].astype(BF16)
        zi_ref[:, gi * ch:(gi + 1) * ch] = uw[:, ch:].astype(BF16)


def _mixer_in(xa, mods, g, p, layer, *, seq, lat_rows, dims):
    rows, d = xa.shape
    heads, rq, rkv, fw, groups = dims
    tm = ROW_TILE
    ncols = p["w_in"].shape[2]
    n_lat = lat_rows // tm
    per_seq = seq // tm
    const2 = lambda i: (0, 0)
    row = lambda i: (i, 0)
    pos = lambda i: (jnp.where(i < n_lat, i % per_seq, per_seq), 0)
    post = lambda i: (0, jnp.where(i < n_lat, i % per_seq, per_seq))
    lay = lambda *shape: pl.BlockSpec((None,) + shape, lambda i: (layer,) + (0,) * len(shape),
                                      pipeline_mode=pl.Buffered(1))
    ch = fw // groups
    out_shape = [
        jax.ShapeDtypeStruct((heads, QK_DIM, rows), BF16),
        jax.ShapeDtypeStruct((heads, rows, QK_DIM), BF16),
        jax.ShapeDtypeStruct((heads, rows // KEY_CHUNK, V_HEAD_DIM, KEY_CHUNK), BF16),
        jax.ShapeDtypeStruct((rows, fw), BF16),
        jax.ShapeDtypeStruct((rows, fw), BF16),
    ]
    out_specs = [
        pl.BlockSpec((heads, QK_DIM, tm), lambda i: (0, 0, i)),
        pl.BlockSpec((heads, tm, QK_DIM), lambda i: (0, i, 0)),
        pl.BlockSpec((heads, tm // KEY_CHUNK, V_HEAD_DIM, KEY_CHUNK), lambda i: (0, i, 0, 0)),
        pl.BlockSpec((tm, fw), row),
        pl.BlockSpec((tm, fw), row),
    ]
    in_specs = [
        pl.BlockSpec((tm, d), row),
        pl.BlockSpec((None, N_MOD, d), lambda i: ((i * tm) // seq, 0, 0)),
        pl.BlockSpec((1, d), const2),
        lay(d, ncols),
        lay(1, rq),
        lay(heads * QK_DIM, rq),
        lay(QK_DIM, 1),
        lay(1, rkv),
        lay(rkv, heads * QK_NOPE_DIM),
        lay(heads * V_HEAD_DIM, rkv),
        lay(1, LANES), lay(1, LANES), lay(1, LANES),
        pl.BlockSpec((tm, LANES), pos), pl.BlockSpec((tm, LANES), pos),
        pl.BlockSpec((QK_ROPE_DIM, tm), post), pl.BlockSpec((QK_ROPE_DIM, tm), post),
        pl.BlockSpec((ch, 2 * ch), const2),
    ]
    return pl.pallas_call(
        functools.partial(_mixer_in_kernel, heads=heads, rq=rq, rkv=rkv, fw=fw, groups=groups),
        grid=(rows // tm,),
        in_specs=in_specs,
        out_specs=out_specs,
        out_shape=out_shape,
        compiler_params=_cparams(("parallel",)),
        name="mixer_in",
    )(xa, mods, g, p["w_in"], p["q_lat_g"], p["w_uqt"], p["gq"], p["kv_lat_g"], p["w_uk"], p["w_uvt"],
      p["gk_n"], p["gk_r"], p["gk_rs"], p["rope_ck"], p["rope_sk"], p["rope_cqt"], p["rope_sqt"], p["dft_ch"])


DFT_BATCH = 16


def _dft_a_kernel(zr_ref, zi_ref, fa_ref, tc_ref, ts_ref, o_ref, *, fw):
    n1 = fa_ref.shape[0] // 2
    fa = fa_ref[...]
    reps = fw // LANES
    for t in range(DFT_BATCH):
        cols = slice(t * fw, (t + 1) * fw)
        zst = jnp.concatenate([zr_ref[:, cols], zi_ref[:, cols]], axis=0)
        z1 = _dot(fa, zst)
        z1r, z1i = z1[:n1], z1[n1:]
        tc = jnp.concatenate([tc_ref[t]] * reps, axis=1)
        ts = jnp.concatenate([ts_ref[t]] * reps, axis=1)
        o_ref[0, :, t, :] = (z1r * tc + z1i * ts).astype(BF16)
        o_ref[1, :, t, :] = (z1i * tc - z1r * ts).astype(BF16)


def _dft_a(zr, zi, p, *, batch, n1, n2, fw):
    rows = zr.shape[0]
    cb = DFT_BATCH
    zr2 = zr.reshape(rows // n2, n2 * fw)
    zi2 = zi.reshape(rows // n2, n2 * fw)
    blk = pl.BlockSpec((n1, cb * fw), lambda b, j: (b, j))
    tw = pl.BlockSpec((cb, n1, LANES), lambda b, j: (j, 0, 0))
    return pl.pallas_call(
        functools.partial(_dft_a_kernel, fw=fw),
        grid=(batch, n2 // cb),
        in_specs=[blk, blk, pl.BlockSpec((2 * n1, 2 * n1), lambda b, j: (0, 0)), tw, tw],
        out_specs=pl.BlockSpec((None, 2, n1, cb, fw), lambda b, j: (b, 0, 0, j, 0)),
        out_shape=jax.ShapeDtypeStruct((batch, 2, n1, n2, fw), BF16),
        compiler_params=_cparams(("parallel", "parallel")),
        name="dft_rows_a",
    )(zr2, zi2, p["dft_a"], p["tw_c"], p["tw_s"])


def _dft_b_kernel(z_ref, fb_ref, o_ref):
    fb = fb_ref[...]
    n2, fw = z_ref.shape[2], z_ref.shape[3]
    for t in range(DFT_BATCH):
        o_ref[:, t, :] = _dot(fb, z_ref[:, t].reshape(2 * n2, fw)).astype(BF16)


def _dft_b(z2, p, *, rows, batch, n1, n2, fw):
    cb = DFT_BATCH
    y = pl.pallas_call(
        _dft_b_kernel,
        grid=(batch, n1 // cb),
        in_specs=[
            pl.BlockSpec((None, 2, cb, n2, fw), lambda b, j: (b, 0, j, 0, 0)),
            pl.BlockSpec((n2, 2 * n2), lambda b, j: (0, 0)),
        ],
        out_specs=pl.BlockSpec((n2, cb, fw), lambda b, j: (b, j, 0)),
        out_shape=jax.ShapeDtypeStruct((rows // n1, n1, fw), BF16),
        compiler_params=_cparams(("parallel", "parallel")),
        name="dft_rows_b",
    )(z2, p["dft_b"])
    return y.reshape(rows, fw)


def _attn_kernel(qt_ref, k_ref, kc_ref, vt_ref, vtc_ref, o_ref, s_ref, acc_ref, *, tk, unroll):
    qt = qt_ref[...]
    per = tk // KEY_CHUNK
    n = k_ref.shape[0] // tk

    def scores(j, slot):
        s = _dot(k_ref[pl.ds(pl.multiple_of(j * tk, tk), tk), :], qt)
        s_ref[slot] = s
        return jnp.max(s, axis=0, keepdims=True)

    def absorb(s, vts, cmax, m, l):
        m_new = jnp.maximum(m, cmax)
        p = jnp.exp2(s - m_new)
        alpha = jnp.exp2(m - m_new)
        l = alpha * l + jnp.sum(p, axis=0, keepdims=True)
        pb = p.astype(BF16)
        pv = _dot(vts[0], pb[:KEY_CHUNK])
        for c in range(1, len(vts)):
            pv = pv + _dot(vts[c], pb[c * KEY_CHUNK:(c + 1) * KEY_CHUNK])
        acc_ref[...] = alpha * acc_ref[...] + pv
        return m_new, l

    def absorb_chunk(j, slot, cmax, m, l):
        return absorb(s_ref[slot], [vt_ref[j * per + c] for c in range(per)], cmax, m, l)

    tq = qt.shape[1]
    acc_ref[...] = jnp.zeros_like(acc_ref)
    sc = _dot(kc_ref[...], qt)
    cmax0 = scores(0, 0)
    m, l = absorb(sc, [vtc_ref[0]], jnp.max(sc, axis=0, keepdims=True),
                  jnp.full((1, tq), -1e30, F32), jnp.zeros((1, tq), F32))

    def group(i, carry):
        m, l, cmax = carry
        for u in range(unroll):
            j = i * unroll + u
            cmax_next = scores(jnp.minimum(j + 1, n - 1), (u + 1) % 2)
            m, l = absorb_chunk(j, u % 2, cmax, m, l)
            cmax = cmax_next
        return m, l, cmax

    m, l, _ = lax.fori_loop(0, n // unroll, group, (m, l, cmax0))
    o_ref[...] = (acc_ref[...] / l).T.astype(BF16)


def _attention(qt, k, vt, *, batch, seq, ctx_len, heads):
    rows = k.shape[1]
    tq, tk = ATT_TQ, ATT_TK
    nq = seq // tq
    lat_rows = batch * seq
    return pl.pallas_call(
        functools.partial(_attn_kernel, tk=tk, unroll=math.gcd(ATT_UNROLL, seq // tk)),
        grid=(batch, heads, nq),
        in_specs=[
            pl.BlockSpec((None, QK_DIM, tq), lambda b, h, i: (h, 0, b * nq + i)),
            pl.BlockSpec((None, seq, QK_DIM), lambda b, h, i: (h, b, 0)),
            pl.BlockSpec((None, ctx_len, QK_DIM), lambda b, h, i: (h, lat_rows // ctx_len + b, 0)),
            pl.BlockSpec((None, seq // KEY_CHUNK, V_HEAD_DIM, KEY_CHUNK), lambda b, h, i: (h, b, 0, 0)),
            pl.BlockSpec((None, 1, V_HEAD_DIM, KEY_CHUNK), lambda b, h, i: (h, lat_rows // KEY_CHUNK + b, 0, 0)),
        ],
        out_specs=pl.BlockSpec((tq, V_HEAD_DIM), lambda b, h, i: (b * nq + i, h)),
        out_shape=jax.ShapeDtypeStruct((rows, heads * V_HEAD_DIM), BF16),
        scratch_shapes=[pltpu.VMEM((2, tk, tq), F32), pltpu.VMEM((V_HEAD_DIM, tq), F32)],
        compiler_params=_cparams(("parallel", "parallel", "arbitrary")),
        name="attention",
    )(qt, k, k, vt, vt)


def _ctx_kernel(qt_ref, k_ref, vt_ref, zr_ref, zi_ref, fc_ref, o_in, y_in, o_ref, y_ref, *, heads):
    del o_in, y_in
    for hd in range(heads):
        s = _dot(k_ref[hd], qt_ref[hd])
        p = jnp.exp2(s - jnp.max(s, axis=0, keepdims=True))
        l = jnp.sum(p, axis=0, keepdims=True)
        o = _dot(vt_ref[hd, 0], p.astype(BF16)) / l
        o_ref[:, hd * V_HEAD_DIM:(hd + 1) * V_HEAD_DIM] = o.T.astype(BF16)
    zst = jnp.concatenate([zr_ref[...], zi_ref[...]], axis=0)
    y_ref[...] = _dot(fc_ref[...], zst).astype(BF16)


def _ctx_mixer(qt, k, vt, zr, zi, o, y, p, *, batch, lat_rows, ctx_len, heads, fw):
    c0 = lat_rows // ctx_len
    return pl.pallas_call(
        functools.partial(_ctx_kernel, heads=heads),
        grid=(batch,),
        in_specs=[
            pl.BlockSpec((heads, QK_DIM, ctx_len), lambda b: (0, 0, c0 + b)),
            pl.BlockSpec((heads, ctx_len, QK_DIM), lambda b: (0, c0 + b, 0)),
            pl.BlockSpec((heads, 1, V_HEAD_DIM, KEY_CHUNK), lambda b: (0, c0 + b, 0, 0)),
            pl.BlockSpec((ctx_len, fw), lambda b: (c0 + b, 0)),
            pl.BlockSpec((ctx_len, fw), lambda b: (c0 + b, 0)),
            pl.BlockSpec((ctx_len, 2 * ctx_len), lambda b: (0, 0)),
            pl.BlockSpec(memory_space=pl.ANY),
            pl.BlockSpec(memory_space=pl.ANY),
        ],
        out_specs=[
            pl.BlockSpec((ctx_len, heads * V_HEAD_DIM), lambda b: (c0 + b, 0)),
            pl.BlockSpec((ctx_len, fw), lambda b: (c0 + b, 0)),
        ],
        out_shape=[jax.ShapeDtypeStruct(o.shape, o.dtype), jax.ShapeDtypeStruct(y.shape, y.dtype)],
        input_output_aliases={6: 0, 7: 1},
        compiler_params=_cparams(("parallel",)),
        name="ctx_mixer",
    )(qt, k, vt, zr, zi, p["dft_ctx"], o, y)


def _mixer_out_kernel(x_ref, mod_ref, a_ref, y_ref, wf_ref, wo_ref, o_ref, *, groups):
    ch = y_ref.shape[1] // groups
    parts = [a_ref[...]]
    for gi in range(groups):
        parts.append(_dot(y_ref[:, gi * ch:(gi + 1) * ch], wf_ref[gi]).astype(BF16))
    cat = jnp.concatenate(parts, axis=1)
    o_ref[...] = x_ref[...] + mod_ref[5:6, :] * _dot(cat, wo_ref[...])


def _mixer_out(xa, mods, o, y, p, layer, *, rows, seq, groups):
    d = xa.shape[1]
    tm = ROW_TILE
    aw, fw = o.shape[1], y.shape[1]
    ch = fw // groups
    row = lambda i: (i, 0)
    return pl.pallas_call(
        functools.partial(_mixer_out_kernel, groups=groups),
        grid=(rows // tm,),
        in_specs=[
            pl.BlockSpec((tm, d), row),
            pl.BlockSpec((None, N_MOD, d), lambda i: ((i * tm) // seq, 0, 0)),
            pl.BlockSpec((tm, aw), row),
            pl.BlockSpec((tm, fw), row),
            pl.BlockSpec((None, groups, ch, ch), lambda i: (layer, 0, 0, 0), pipeline_mode=pl.Buffered(1)),
            pl.BlockSpec((None, aw + fw, d), lambda i: (layer, 0, 0), pipeline_mode=pl.Buffered(1)),
        ],
        out_specs=pl.BlockSpec((tm, d), row),
        out_shape=jax.ShapeDtypeStruct((rows, d), F32),
        compiler_params=_cparams(("parallel",)),
        name="mixer_out",
    )(xa, mods, o, y, p["w_f"], p["w_o"])


def _dft_parts(n):
    idx = np.arange(n, dtype=np.int64)
    ang = 2.0 * np.pi * ((idx[:, None] * idx[None, :]) % n).astype(np.float64) / n
    s = 1.0 / math.sqrt(n)
    return np.cos(ang) * s, np.sin(ang) * s


def _tables(seq, ctx_len, ch, n1, n2, tm):
    t = {}
    c, s = _dft_parts(ch)
    t["dft_ch"] = jnp.asarray(np.concatenate([c, -s], axis=1), BF16)
    c, s = _dft_parts(n1)
    t["dft_a"] = jnp.asarray(np.block([[c, s], [-s, c]]), BF16)
    c, s = _dft_parts(n2)
    t["dft_b"] = jnp.asarray(np.concatenate([c, s], axis=1), BF16)
    c, s = _dft_parts(ctx_len)
    t["dft_ctx"] = jnp.asarray(np.concatenate([c, s], axis=1), BF16)
    i2 = np.arange(n2, dtype=np.int64)[:, None]
    k1 = np.arange(n1, dtype=np.int64)[None, :]
    ang = 2.0 * np.pi * ((i2 * k1) % seq).astype(np.float64) / seq
    t["tw_c"] = jnp.asarray(np.broadcast_to(np.cos(ang)[:, :, None], (n2, n1, LANES)), F32)
    t["tw_s"] = jnp.asarray(np.broadcast_to(np.sin(ang)[:, :, None], (n2, n1, LANES)), F32)

    rows = seq // GRID_W
    row = jnp.broadcast_to(jnp.arange(rows, dtype=F32)[:, None], (rows, GRID_W)).reshape(-1)
    col = jnp.broadcast_to(jnp.arange(GRID_W, dtype=F32)[None, :], (rows, GRID_W)).reshape(-1)
    inv_freq = ROPE_THETA ** (-jnp.arange(AXIS_FREQS, dtype=F32) / AXIS_FREQS)
    ang_r = row[:, None] * inv_freq
    ang_c = col[:, None] * inv_freq
    cr, sr, cc, sc = jnp.cos(ang_r), jnp.sin(ang_r), jnp.cos(ang_c), jnp.sin(ang_c)
    cos = jnp.concatenate([cr, cr, cc, cc], axis=1)
    sin = jnp.concatenate([-sr, sr, -sc, sc], axis=1)
    cos = jnp.concatenate([cos, jnp.ones((tm, QK_ROPE_DIM), F32)], axis=0)
    sin = jnp.concatenate([sin, jnp.zeros((tm, QK_ROPE_DIM), F32)], axis=0)
    pad = ((0, 0), (0, LANES - QK_ROPE_DIM))
    t["rope_ck"] = jnp.pad(cos, pad)
    t["rope_sk"] = jnp.pad(sin, pad)
    t["rope_cqt"] = cos.T
    t["rope_sqt"] = sin.T
    return t


def _swap_perm():
    a = AXIS_FREQS
    return np.concatenate([np.arange(a, 2 * a), np.arange(0, a), np.arange(3 * a, 4 * a), np.arange(2 * a, 3 * a)])


def kernel(x, c, ctx, c_ctx, w_ada, b_ada, norm_g, ffn1_w_gate, ffn1_w_up, ffn1_w_down, ffn2_w_gate, ffn2_w_up,
           ffn2_w_down, w_in, q_lat_g, w_uq, kv_lat_g, w_ukv, q_norm_g, k_norm_g, w_fourier, w_o):
    batch, seq, d = x.shape
    ctx_len = ctx.shape[1]
    depth = w_ada.shape[0]
    rq = q_lat_g.shape[1]
    rkv = kv_lat_g.shape[1]
    heads = w_uq.shape[2] // QK_DIM
    groups = w_fourier.shape[1]
    ch = w_fourier.shape[2]
    fw = groups * ch
    tm = ROW_TILE
    lat_rows = batch * seq
    ctx_rows = batch * ctx_len
    rows = lat_rows + ctx_rows
    n1 = 1 << ((seq.bit_length() - 1) // 2)
    n2 = seq // n1
    assert n1 * n2 == seq and n1 % DFT_BATCH == 0 and n2 % DFT_BATCH == 0
    assert ctx_len == KEY_CHUNK and seq % tm == 0 and ctx_rows % tm == 0 and ctx_rows <= seq
    assert rows % n1 == 0 and rows % n2 == 0 and seq % ATT_TQ == 0 and seq % (2 * ATT_TK) == 0
    assert batch + 1 <= COND_ROWS and w_in.shape[2] == rq + rkv + QK_ROPE_DIM + fw
    assert w_ukv.shape[2] == heads * (QK_NOPE_DIM + V_HEAD_DIM)

    perm = _swap_perm()
    k0 = rq + rkv
    zpad = jnp.zeros((depth, d, LANES - QK_ROPE_DIM), w_in.dtype)
    w_rope = w_in[:, :, k0:k0 + QK_ROPE_DIM]
    w_in_ext = jnp.concatenate(
        [w_in[:, :, :k0], w_in[:, :, k0 + QK_ROPE_DIM:], w_rope, zpad, w_rope[:, :, perm], zpad], axis=2)
    w_ukv4 = w_ukv.reshape(depth, rkv, heads, QK_NOPE_DIM + V_HEAD_DIM)
    gpad = ((0, 0), (0, 0), (0, LANES - QK_ROPE_DIM))
    g_rope = k_norm_g[:, None, QK_NOPE_DIM:]
    p = _tables(seq, ctx_len, ch, n1, n2, tm)
    p.update(
        w_in=w_in_ext.astype(BF16),
        q_lat_g=q_lat_g[:, None, :],
        w_uqt=jnp.swapaxes(w_uq, 1, 2).astype(BF16),
        gq=q_norm_g[:, :, None],
        kv_lat_g=kv_lat_g[:, None, :],
        w_uk=w_ukv4[..., :QK_NOPE_DIM].reshape(depth, rkv, heads * QK_NOPE_DIM).astype(BF16),
        w_uvt=jnp.swapaxes(w_ukv4[..., QK_NOPE_DIM:].reshape(depth, rkv, heads * V_HEAD_DIM), 1, 2).astype(BF16),
        gk_n=k_norm_g[:, None, :QK_NOPE_DIM],
        gk_r=jnp.pad(g_rope, gpad),
        gk_rs=jnp.pad(g_rope[:, :, perm], gpad),
        w_f=w_fourier.astype(BF16),
        w_o=w_o.astype(BF16),
    )
    ffn1 = tuple(w.astype(BF16) for w in (ffn1_w_gate, ffn1_w_up, ffn1_w_down))
    ffn2 = tuple(w.astype(BF16) for w in (ffn2_w_gate, ffn2_w_up, ffn2_w_down))

    cond = jnp.concatenate([c, c_ctx[None, :], jnp.zeros((COND_ROWS - batch - 1, d), c.dtype)], axis=0)
    mods_all = _adaln(cond, w_ada, b_ada).reshape(depth, COND_ROWS, N_MOD, d)

    xa = jnp.concatenate([x.reshape(lat_rows, d), ctx.reshape(ctx_rows, d)], axis=0)
    dims = (heads, rq, rkv, fw, groups)
    for layer in range(depth):
        last = layer == depth - 1
        mods = mods_all[layer]
        g = norm_g[layer]
        xa = _ffn(xa, mods, g[0:1], *ffn1, layer, mod0=0, rows=rows, seq=seq)
        qt, k, vt, zr, zi = _mixer_in(xa, mods, g[1:2], p, layer, seq=seq, lat_rows=lat_rows, dims=dims)
        z2 = _dft_a(zr, zi, p, batch=batch, n1=n1, n2=n2, fw=fw)
        y = _dft_b(z2, p, rows=rows, batch=batch, n1=n1, n2=n2, fw=fw)
        o = _attention(qt, k, vt, batch=batch, seq=seq, ctx_len=ctx_len, heads=heads)
        if not last:
            o, y = _ctx_mixer(qt, k, vt, zr, zi, o, y, p, batch=batch, lat_rows=lat_rows, ctx_len=ctx_len,
                              heads=heads, fw=fw)
        out_rows = lat_rows if last else rows
        xa = _mixer_out(xa, mods, o, y, p, layer, rows=out_rows, seq=seq, groups=groups)
        xa = _ffn(xa, mods, g[2:3], *ffn2, layer, mod0=6, rows=out_rows, seq=seq)
    return xa.reshape(batch, seq, d)
```

```python
import functools
import math

import jax
import jax.numpy as jnp
import numpy as np
from jax import lax
from jax.experimental import pallas as pl
from jax.experimental.pallas import tpu as pltpu

GRID_W = 64
V_HEAD_DIM = 128
QK_NOPE_DIM = 128
QK_ROPE_DIM = 64
QK_DIM = QK_NOPE_DIM + QK_ROPE_DIM
FOURIER_GROUPS = 4
N_MOD = 9
AXIS_FREQS = QK_ROPE_DIM // 4
ROPE_THETA = 10000.0
SM_SCALE = QK_DIM ** -0.5
RMS_EPS = 1e-6
LOG2E = 1.4426950408889634

LANES = 128
COND_ROWS = 8
VMEM_LIMIT = 56 * 1024 * 1024

ROW_TILE = 512
KEY_CHUNK = 256
FFN_TILE = 512
ADA_TILE = 1024
ATT_TQ = 512
ATT_TK = 512
ATT_UNROLL = 16

BF16 = jnp.bfloat16
F32 = jnp.float32


def _dot(a, b):
    return jnp.dot(a, b, preferred_element_type=F32)


def _dot_nt(a, b):
    return lax.dot_general(a, b, (((1,), (1,)), ((), ())), preferred_element_type=F32)


def _cparams(sem):
    return pltpu.CompilerParams(dimension_semantics=sem, vmem_limit_bytes=VMEM_LIMIT)


def _norm_mod(x, g, shift, scale):
    y = x * lax.rsqrt(jnp.mean(x * x, axis=-1, keepdims=True) + RMS_EPS) * g
    return y * (1.0 + scale) + shift


def _adaln_kernel(c_ref, w_ref, b_ref, o_ref):
    c = c_ref[...]
    s = (c * jax.nn.sigmoid(c)).astype(BF16)
    o_ref[...] = _dot(s, w_ref[...].astype(BF16)) + b_ref[...]


def _adaln(cond, w_ada, b_ada):
    depth, d, n = w_ada.shape
    tn = math.gcd(ADA_TILE, d)
    return pl.pallas_call(
        _adaln_kernel,
        grid=(depth, n // tn),
        in_specs=[
            pl.BlockSpec((COND_ROWS, d), lambda l, j: (0, 0)),
            pl.BlockSpec((None, d, tn), lambda l, j: (l, 0, j)),
            pl.BlockSpec((None, 1, tn), lambda l, j: (l, 0, j)),
        ],
        out_specs=pl.BlockSpec((None, COND_ROWS, tn), lambda l, j: (l, 0, j)),
        out_shape=jax.ShapeDtypeStruct((depth, COND_ROWS, n), F32),
        compiler_params=_cparams(("arbitrary", "arbitrary")),
        name="adaln",
    )(cond, w_ada, b_ada.reshape(depth, 1, n))


def _ffn_kernel(x_ref, mod_ref, g_ref, wg_ref, wu_ref, wd_ref, o_ref, h_ref, acc_ref, *, mod0):
    f = pl.program_id(1)

    @pl.when(f == 0)
    def _():
        h = _norm_mod(x_ref[...], g_ref[...], mod_ref[mod0:mod0 + 1, :], mod_ref[mod0 + 1:mod0 + 2, :])
        h_ref[...] = h.astype(BF16)
        acc_ref[...] = jnp.zeros_like(acc_ref)

    h = h_ref[...]
    gate = _dot(h, wg_ref[...])
    up = _dot(h, wu_ref[...])
    a = (gate * jax.nn.sigmoid(gate) * up).astype(BF16)
    acc_ref[...] += _dot(a, wd_ref[...])

    @pl.when(f == pl.num_programs(1) - 1)
    def _():
        o_ref[...] = x_ref[...] + (0.5 * mod_ref[mod0 + 2:mod0 + 3, :]) * acc_ref[...]


def _ffn(xa, mods, g, wg, wu, wd, layer, *, mod0, rows, seq):
    d = xa.shape[1]
    d_ff = wg.shape[2]
    tm, tf = ROW_TILE, FFN_TILE
    row = lambda i, f: (i, 0)
    return pl.pallas_call(
        functools.partial(_ffn_kernel, mod0=mod0),
        grid=(rows // tm, d_ff // tf),
        in_specs=[
            pl.BlockSpec((tm, d), row),
            pl.BlockSpec((None, N_MOD, d), lambda i, f: ((i * tm) // seq, 0, 0)),
            pl.BlockSpec((1, d), lambda i, f: (0, 0)),
            pl.BlockSpec((None, d, tf), lambda i, f: (layer, 0, f)),
            pl.BlockSpec((None, d, tf), lambda i, f: (layer, 0, f)),
            pl.BlockSpec((None, tf, d), lambda i, f: (layer, f, 0)),
        ],
        out_specs=pl.BlockSpec((tm, d), row),
        out_shape=jax.ShapeDtypeStruct((rows, d), F32),
        scratch_shapes=[pltpu.VMEM((tm, d), BF16), pltpu.VMEM((tm, d), F32)],
        compiler_params=_cparams(("parallel", "arbitrary")),
        name="ffn",
    )(xa, mods, g, wg, wu, wd)


def _mixer_in_kernel(x_ref, mod_ref, g_ref, win_ref, gql_ref, wuqt_ref, gq_ref, gkvl_ref, wuk_ref, wuvt_ref,
                     gkn_ref, gkr_ref, gkrs_ref, ck_ref, sk_ref, cq_ref, sq_ref, dft_ref,
                     qt_ref, k_ref, vt_ref, zr_ref, zi_ref, *, heads, rq, rkv, fw, groups):
    tm = x_ref.shape[0]
    h = _norm_mod(x_ref[...], g_ref[...], mod_ref[3:4, :], mod_ref[4:5, :]).astype(BF16)
    proj = _dot(h, win_ref[...])
    f0 = rq + rkv
    r0 = f0 + fw

    qlat = proj[:, :rq]
    qn = (qlat * lax.rsqrt(jnp.mean(qlat * qlat, axis=-1, keepdims=True) + RMS_EPS) * gql_ref[...]).astype(BF16)
    qt = _dot_nt(wuqt_ref[...], qn)
    gq = gq_ref[...]
    cq = cq_ref[...]
    sq = sq_ref[...]
    a = AXIS_FREQS
    for hd in range(heads):
        blk = qt[hd * QK_DIM:(hd + 1) * QK_DIM]
        r = lax.rsqrt(jnp.mean(blk * blk, axis=0, keepdims=True) + RMS_EPS) * (SM_SCALE * LOG2E)
        blk = blk * r * gq
        rp = blk[QK_NOPE_DIM:]
        sw = jnp.concatenate([rp[a:2 * a], rp[:a], rp[3 * a:], rp[2 * a:3 * a]], axis=0)
        qt_ref[hd, :QK_NOPE_DIM, :] = blk[:QK_NOPE_DIM].astype(BF16)
        qt_ref[hd, QK_NOPE_DIM:, :] = (rp * cq + sw * sq).astype(BF16)

    kvlat = proj[:, rq:f0]
    kvn = (kvlat * lax.rsqrt(jnp.mean(kvlat * kvlat, axis=-1, keepdims=True) + RMS_EPS) * gkvl_ref[...]).astype(BF16)
    knope = _dot(kvn, wuk_ref[...])
    vt = _dot_nt(wuvt_ref[...], kvn)
    for c in range(tm // KEY_CHUNK):
        piece = vt[:, c * KEY_CHUNK:(c + 1) * KEY_CHUNK].astype(BF16)
        vt_ref[:, c] = piece.reshape(heads, V_HEAD_DIM, KEY_CHUNK)
    kr = proj[:, r0:r0 + LANES]
    krs = proj[:, r0 + LANES:r0 + 2 * LANES]
    krr = kr * gkr_ref[...] * ck_ref[...] + krs * gkrs_ref[...] * sk_ref[...]
    ss_r = jnp.sum(kr * kr, axis=-1, keepdims=True)
    gkn = gkn_ref[...]
    for hd in range(heads):
        kn = knope[:, hd * QK_NOPE_DIM:(hd + 1) * QK_NOPE_DIM]
        ss = jnp.sum(kn * kn, axis=-1, keepdims=True) + ss_r
        r = lax.rsqrt(ss * (1.0 / QK_DIM) + RMS_EPS)
        k_ref[hd, :, :QK_NOPE_DIM] = (kn * r * gkn).astype(BF16)
        k_ref[hd, :, QK_NOPE_DIM:] = (krr * r)[:, :QK_ROPE_DIM].astype(BF16)

    ch = fw // groups
    fb = proj[:, f0:r0].astype(BF16)
    dft = dft_ref[...]
    for gi in range(groups):
        uw = _dot(fb[:, gi * ch:(gi + 1) * ch], dft)
        zr_ref[:, gi * ch:(gi + 1) * ch] = uw[:, :ch].astype(BF16)
        zi_ref[:, gi * ch:(gi + 1) * ch] = uw[:, ch:].astype(BF16)


def _mixer_in(xa, mods, g, p, layer, *, seq, lat_rows, dims):
    rows, d = xa.shape
    heads, rq, rkv, fw, groups = dims
    tm = ROW_TILE
    ncols = p["w_in"].shape[2]
    n_lat = lat_rows // tm
    per_seq = seq // tm
    const2 = lambda i: (0, 0)
    row = lambda i: (i, 0)
    pos = lambda i: (jnp.where(i < n_lat, i % per_seq, per_seq), 0)
    post = lambda i: (0, jnp.where(i < n_lat, i % per_seq, per_seq))
    lay = lambda *shape: pl.BlockSpec((None,) + shape, lambda i: (layer,) + (0,) * len(shape),
                                      pipeline_mode=pl.Buffered(1))
    ch = fw // groups
    out_shape = [
        jax.ShapeDtypeStruct((heads, QK_DIM, rows), BF16),
        jax.ShapeDtypeStruct((heads, rows, QK_DIM), BF16),
        jax.ShapeDtypeStruct((heads, rows // KEY_CHUNK, V_HEAD_DIM, KEY_CHUNK), BF16),
        jax.ShapeDtypeStruct((rows, fw), BF16),
        jax.ShapeDtypeStruct((rows, fw), BF16),
    ]
    out_specs = [
        pl.BlockSpec((heads, QK_DIM, tm), lambda i: (0, 0, i)),
        pl.BlockSpec((heads, tm, QK_DIM), lambda i: (0, i, 0)),
        pl.BlockSpec((heads, tm // KEY_CHUNK, V_HEAD_DIM, KEY_CHUNK), lambda i: (0, i, 0, 0)),
        pl.BlockSpec((tm, fw), row),
        pl.BlockSpec((tm, fw), row),
    ]
    in_specs = [
        pl.BlockSpec((tm, d), row),
        pl.BlockSpec((None, N_MOD, d), lambda i: ((i * tm) // seq, 0, 0)),
        pl.BlockSpec((1, d), const2),
        lay(d, ncols),
        lay(1, rq),
        lay(heads * QK_DIM, rq),
        lay(QK_DIM, 1),
        lay(1, rkv),
        lay(rkv, heads * QK_NOPE_DIM),
        lay(heads * V_HEAD_DIM, rkv),
        lay(1, LANES), lay(1, LANES), lay(1, LANES),
        pl.BlockSpec((tm, LANES), pos), pl.BlockSpec((tm, LANES), pos),
        pl.BlockSpec((QK_ROPE_DIM, tm), post), pl.BlockSpec((QK_ROPE_DIM, tm), post),
        pl.BlockSpec((ch, 2 * ch), const2),
    ]
    return pl.pallas_call(
        functools.partial(_mixer_in_kernel, heads=heads, rq=rq, rkv=rkv, fw=fw, groups=groups),
        grid=(rows // tm,),
        in_specs=in_specs,
        out_specs=out_specs,
        out_shape=out_shape,
        compiler_params=_cparams(("parallel",)),
        name="mixer_in",
    )(xa, mods, g, p["w_in"], p["q_lat_g"], p["w_uqt"], p["gq"], p["kv_lat_g"], p["w_uk"], p["w_uvt"],
      p["gk_n"], p["gk_r"], p["gk_rs"], p["rope_ck"], p["rope_sk"], p["rope_cqt"], p["rope_sqt"], p["dft_ch"])


DFT_BATCH = 16


def _dft_a_kernel(zr_ref, zi_ref, fa_ref, tc_ref, ts_ref, o_ref, *, fw):
    n1 = fa_ref.shape[0] // 2
    fa = fa_ref[...]
    reps = fw // LANES
    for t in range(DFT_BATCH):
        cols = slice(t * fw, (t + 1) * fw)
        zst = jnp.concatenate([zr_ref[:, cols], zi_ref[:, cols]], axis=0)
        z1 = _dot(fa, zst)
        z1r, z1i = z1[:n1], z1[n1:]
        tc = jnp.concatenate([tc_ref[t]] * reps, axis=1)
        ts = jnp.concatenate([ts_ref[t]] * reps, axis=1)
        o_ref[0, :, t, :] = (z1r * tc + z1i * ts).astype(BF16)
        o_ref[1, :, t, :] = (z1i * tc - z1r * ts).astype(BF16)


def _dft_a(zr, zi, p, *, batch, n1, n2, fw):
    rows = zr.shape[0]
    cb = DFT_BATCH
    zr2 = zr.reshape(rows // n2, n2 * fw)
    zi2 = zi.reshape(rows // n2, n2 * fw)
    blk = pl.BlockSpec((n1, cb * fw), lambda b, j: (b, j))
    tw = pl.BlockSpec((cb, n1, LANES), lambda b, j: (j, 0, 0))
    return pl.pallas_call(
        functools.partial(_dft_a_kernel, fw=fw),
        grid=(batch, n2 // cb),
        in_specs=[blk, blk, pl.BlockSpec((2 * n1, 2 * n1), lambda b, j: (0, 0)), tw, tw],
        out_specs=pl.BlockSpec((None, 2, n1, cb, fw), lambda b, j: (b, 0, 0, j, 0)),
        out_shape=jax.ShapeDtypeStruct((batch, 2, n1, n2, fw), BF16),
        compiler_params=_cparams(("parallel", "parallel")),
        name="dft_rows_a",
    )(zr2, zi2, p["dft_a"], p["tw_c"], p["tw_s"])


def _dft_b_kernel(z_ref, fb_ref, o_ref):
    fb = fb_ref[...]
    n2, fw = z_ref.shape[2], z_ref.shape[3]
    for t in range(DFT_BATCH):
        o_ref[:, t, :] = _dot(fb, z_ref[:, t].reshape(2 * n2, fw)).astype(BF16)


def _dft_b(z2, p, *, rows, batch, n1, n2, fw):
    cb = DFT_BATCH
    y = pl.pallas_call(
        _dft_b_kernel,
        grid=(batch, n1 // cb),
        in_specs=[
            pl.BlockSpec((None, 2, cb, n2, fw), lambda b, j: (b, 0, j, 0, 0)),
            pl.BlockSpec((n2, 2 * n2), lambda b, j: (0, 0)),
        ],
        out_specs=pl.BlockSpec((n2, cb, fw), lambda b, j: (b, j, 0)),
        out_shape=jax.ShapeDtypeStruct((rows // n1, n1, fw), BF16),
        compiler_params=_cparams(("parallel", "parallel")),
        name="dft_rows_b",
    )(z2, p["dft_b"])
    return y.reshape(rows, fw)


def _attn_kernel(qt_ref, k_ref, kc_ref, vt_ref, vtc_ref, o_ref, s_ref, acc_ref, *, tk, unroll):
    qt = qt_ref[...]
    per = tk // KEY_CHUNK
    n = k_ref.shape[0] // tk

    def scores(j, slot):
        s = _dot(k_ref[pl.ds(pl.multiple_of(j * tk, tk), tk), :], qt)
        s_ref[slot] = s
        return jnp.max(s, axis=0, keepdims=True)

    def absorb(s, vts, cmax, m, l):
        m_new = jnp.maximum(m, cmax)
        p = jnp.exp2(s - m_new)
        alpha = jnp.exp2(m - m_new)
        l = alpha * l + jnp.sum(p, axis=0, keepdims=True)
        vt = vts[0] if len(vts) == 1 else jnp.concatenate(vts, axis=1)
        acc_ref[...] = alpha * acc_ref[...] + _dot(vt, p.astype(BF16))
        return m_new, l

    def absorb_chunk(j, slot, cmax, m, l):
        return absorb(s_ref[slot], [vt_ref[j * per + c] for c in range(per)], cmax, m, l)

    tq = qt.shape[1]
    acc_ref[...] = jnp.zeros_like(acc_ref)
    sc = _dot(kc_ref[...], qt)
    cmax0 = scores(0, 0)
    m, l = absorb(sc, [vtc_ref[0]], jnp.max(sc, axis=0, keepdims=True),
                  jnp.full((1, tq), -1e30, F32), jnp.zeros((1, tq), F32))

    def group(i, carry):
        m, l, cmax = carry
        for u in range(unroll):
            j = i * unroll + u
            cmax_next = scores(jnp.minimum(j + 1, n - 1), (u + 1) % 2)
            m, l = absorb_chunk(j, u % 2, cmax, m, l)
            cmax = cmax_next
        return m, l, cmax

    m, l, _ = lax.fori_loop(0, n // unroll, group, (m, l, cmax0))
    o_ref[...] = (acc_ref[...] / l).T.astype(BF16)


def _attention(qt, k, vt, *, batch, seq, ctx_len, heads):
    rows = k.shape[1]
    tq, tk = ATT_TQ, ATT_TK
    nq = seq // tq
    lat_rows = batch * seq
    return pl.pallas_call(
        functools.partial(_attn_kernel, tk=tk, unroll=math.gcd(ATT_UNROLL, seq // tk)),
        grid=(batch, heads, nq),
        in_specs=[
            pl.BlockSpec((None, QK_DIM, tq), lambda b, h, i: (h, 0, b * nq + i)),
            pl.BlockSpec((None, seq, QK_DIM), lambda b, h, i: (h, b, 0)),
            pl.BlockSpec((None, ctx_len, QK_DIM), lambda b, h, i: (h, lat_rows // ctx_len + b, 0)),
            pl.BlockSpec((None, seq // KEY_CHUNK, V_HEAD_DIM, KEY_CHUNK), lambda b, h, i: (h, b, 0, 0)),
            pl.BlockSpec((None, 1, V_HEAD_DIM, KEY_CHUNK), lambda b, h, i: (h, lat_rows // KEY_CHUNK + b, 0, 0)),
        ],
        out_specs=pl.BlockSpec((tq, V_HEAD_DIM), lambda b, h, i: (b * nq + i, h)),
        out_shape=jax.ShapeDtypeStruct((rows, heads * V_HEAD_DIM), BF16),
        scratch_shapes=[pltpu.VMEM((2, tk, tq), F32), pltpu.VMEM((V_HEAD_DIM, tq), F32)],
        compiler_params=_cparams(("parallel", "parallel", "arbitrary")),
        name="attention",
    )(qt, k, k, vt, vt)


def _ctx_kernel(qt_ref, k_ref, vt_ref, zr_ref, zi_ref, fc_ref, o_in, y_in, o_ref, y_ref, *, heads):
    del o_in, y_in
    for hd in range(heads):
        s = _dot(k_ref[hd], qt_ref[hd])
        p = jnp.exp2(s - jnp.max(s, axis=0, keepdims=True))
        l = jnp.sum(p, axis=0, keepdims=True)
        o = _dot(vt_ref[hd, 0], p.astype(BF16)) / l
        o_ref[:, hd * V_HEAD_DIM:(hd + 1) * V_HEAD_DIM] = o.T.astype(BF16)
    zst = jnp.concatenate([zr_ref[...], zi_ref[...]], axis=0)
    y_ref[...] = _dot(fc_ref[...], zst).astype(BF16)


def _ctx_mixer(qt, k, vt, zr, zi, o, y, p, *, batch, lat_rows, ctx_len, heads, fw):
    c0 = lat_rows // ctx_len
    return pl.pallas_call(
        functools.partial(_ctx_kernel, heads=heads),
        grid=(batch,),
        in_specs=[
            pl.BlockSpec((heads, QK_DIM, ctx_len), lambda b: (0, 0, c0 + b)),
            pl.BlockSpec((heads, ctx_len, QK_DIM), lambda b: (0, c0 + b, 0)),
            pl.BlockSpec((heads, 1, V_HEAD_DIM, KEY_CHUNK), lambda b: (0, c0 + b, 0, 0)),
            pl.BlockSpec((ctx_len, fw), lambda b: (c0 + b, 0)),
            pl.BlockSpec((ctx_len, fw), lambda b: (c0 + b, 0)),
            pl.BlockSpec((ctx_len, 2 * ctx_len), lambda b: (0, 0)),
            pl.BlockSpec(memory_space=pl.ANY),
            pl.BlockSpec(memory_space=pl.ANY),
        ],
        out_specs=[
            pl.BlockSpec((ctx_len, heads * V_HEAD_DIM), lambda b: (c0 + b, 0)),
            pl.BlockSpec((ctx_len, fw), lambda b: (c0 + b, 0)),
        ],
        out_shape=[jax.ShapeDtypeStruct(o.shape, o.dtype), jax.ShapeDtypeStruct(y.shape, y.dtype)],
        input_output_aliases={6: 0, 7: 1},
        compiler_params=_cparams(("parallel",)),
        name="ctx_mixer",
    )(qt, k, vt, zr, zi, p["dft_ctx"], o, y)


def _mixer_out_kernel(x_ref, mod_ref, a_ref, y_ref, wf_ref, wo_ref, o_ref, *, groups):
    ch = y_ref.shape[1] // groups
    parts = [a_ref[...]]
    for gi in range(groups):
        parts.append(_dot(y_ref[:, gi * ch:(gi + 1) * ch], wf_ref[gi]).astype(BF16))
    cat = jnp.concatenate(parts, axis=1)
    o_ref[...] = x_ref[...] + mod_ref[5:6, :] * _dot(cat, wo_ref[...])


def _mixer_out(xa, mods, o, y, p, layer, *, rows, seq, groups):
    d = xa.shape[1]
    tm = ROW_TILE
    aw, fw = o.shape[1], y.shape[1]
    ch = fw // groups
    row = lambda i: (i, 0)
    return pl.pallas_call(
        functools.partial(_mixer_out_kernel, groups=groups),
        grid=(rows // tm,),
        in_specs=[
            pl.BlockSpec((tm, d), row),
            pl.BlockSpec((None, N_MOD, d), lambda i: ((i * tm) // seq, 0, 0)),
            pl.BlockSpec((tm, aw), row),
            pl.BlockSpec((tm, fw), row),
            pl.BlockSpec((None, groups, ch, ch), lambda i: (layer, 0, 0, 0), pipeline_mode=pl.Buffered(1)),
            pl.BlockSpec((None, aw + fw, d), lambda i: (layer, 0, 0), pipeline_mode=pl.Buffered(1)),
        ],
        out_specs=pl.BlockSpec((tm, d), row),
        out_shape=jax.ShapeDtypeStruct((rows, d), F32),
        compiler_params=_cparams(("parallel",)),
        name="mixer_out",
    )(xa, mods, o, y, p["w_f"], p["w_o"])


def _dft_parts(n):
    idx = np.arange(n, dtype=np.int64)
    ang = 2.0 * np.pi * ((idx[:, None] * idx[None, :]) % n).astype(np.float64) / n
    s = 1.0 / math.sqrt(n)
    return np.cos(ang) * s, np.sin(ang) * s


def _tables(seq, ctx_len, ch, n1, n2, tm):
    t = {}
    c, s = _dft_parts(ch)
    t["dft_ch"] = jnp.asarray(np.concatenate([c, -s], axis=1), BF16)
    c, s = _dft_parts(n1)
    t["dft_a"] = jnp.asarray(np.block([[c, s], [-s, c]]), BF16)
    c, s = _dft_parts(n2)
    t["dft_b"] = jnp.asarray(np.concatenate([c, s], axis=1), BF16)
    c, s = _dft_parts(ctx_len)
    t["dft_ctx"] = jnp.asarray(np.concatenate([c, s], axis=1), BF16)
    i2 = np.arange(n2, dtype=np.int64)[:, None]
    k1 = np.arange(n1, dtype=np.int64)[None, :]
    ang = 2.0 * np.pi * ((i2 * k1) % seq).astype(np.float64) / seq
    t["tw_c"] = jnp.asarray(np.broadcast_to(np.cos(ang)[:, :, None], (n2, n1, LANES)), F32)
    t["tw_s"] = jnp.asarray(np.broadcast_to(np.sin(ang)[:, :, None], (n2, n1, LANES)), F32)

    rows = seq // GRID_W
    row = jnp.broadcast_to(jnp.arange(rows, dtype=F32)[:, None], (rows, GRID_W)).reshape(-1)
    col = jnp.broadcast_to(jnp.arange(GRID_W, dtype=F32)[None, :], (rows, GRID_W)).reshape(-1)
    inv_freq = ROPE_THETA ** (-jnp.arange(AXIS_FREQS, dtype=F32) / AXIS_FREQS)
    ang_r = row[:, None] * inv_freq
    ang_c = col[:, None] * inv_freq
    cr, sr, cc, sc = jnp.cos(ang_r), jnp.sin(ang_r), jnp.cos(ang_c), jnp.sin(ang_c)
    cos = jnp.concatenate([cr, cr, cc, cc], axis=1)
    sin = jnp.concatenate([-sr, sr, -sc, sc], axis=1)
    cos = jnp.concatenate([cos, jnp.ones((tm, QK_ROPE_DIM), F32)], axis=0)
    sin = jnp.concatenate([sin, jnp.zeros((tm, QK_ROPE_DIM), F32)], axis=0)
    pad = ((0, 0), (0, LANES - QK_ROPE_DIM))
    t["rope_ck"] = jnp.pad(cos, pad)
    t["rope_sk"] = jnp.pad(sin, pad)
    t["rope_cqt"] = cos.T
    t["rope_sqt"] = sin.T
    return t


def _swap_perm():
    a = AXIS_FREQS
    return np.concatenate([np.arange(a, 2 * a), np.arange(0, a), np.arange(3 * a, 4 * a), np.arange(2 * a, 3 * a)])


def kernel(x, c, ctx, c_ctx, w_ada, b_ada, norm_g, ffn1_w_gate, ffn1_w_up, ffn1_w_down, ffn2_w_gate, ffn2_w_up,
           ffn2_w_down, w_in, q_lat_g, w_uq, kv_lat_g, w_ukv, q_norm_g, k_norm_g, w_fourier, w_o):
    batch, seq, d = x.shape
    ctx_len = ctx.shape[1]
    depth = w_ada.shape[0]
    rq = q_lat_g.shape[1]
    rkv = kv_lat_g.shape[1]
    heads = w_uq.shape[2] // QK_DIM
    groups = w_fourier.shape[1]
    ch = w_fourier.shape[2]
    fw = groups * ch
    tm = ROW_TILE
    lat_rows = batch * seq
    ctx_rows = batch * ctx_len
    rows = lat_rows + ctx_rows
    n1 = 1 << ((seq.bit_length() - 1) // 2)
    n2 = seq // n1
    assert n1 * n2 == seq and n1 % DFT_BATCH == 0 and n2 % DFT_BATCH == 0
    assert ctx_len == KEY_CHUNK and seq % tm == 0 and ctx_rows % tm == 0 and ctx_rows <= seq
    assert rows % n1 == 0 and rows % n2 == 0 and seq % ATT_TQ == 0 and seq % (2 * ATT_TK) == 0
    assert batch + 1 <= COND_ROWS and w_in.shape[2] == rq + rkv + QK_ROPE_DIM + fw
    assert w_ukv.shape[2] == heads * (QK_NOPE_DIM + V_HEAD_DIM)

    perm = _swap_perm()
    k0 = rq + rkv
    zpad = jnp.zeros((depth, d, LANES - QK_ROPE_DIM), w_in.dtype)
    w_rope = w_in[:, :, k0:k0 + QK_ROPE_DIM]
    w_in_ext = jnp.concatenate(
        [w_in[:, :, :k0], w_in[:, :, k0 + QK_ROPE_DIM:], w_rope, zpad, w_rope[:, :, perm], zpad], axis=2)
    w_ukv4 = w_ukv.reshape(depth, rkv, heads, QK_NOPE_DIM + V_HEAD_DIM)
    gpad = ((0, 0), (0, 0), (0, LANES - QK_ROPE_DIM))
    g_rope = k_norm_g[:, None, QK_NOPE_DIM:]
    p = _tables(seq, ctx_len, ch, n1, n2, tm)
    p.update(
        w_in=w_in_ext.astype(BF16),
        q_lat_g=q_lat_g[:, None, :],
        w_uqt=jnp.swapaxes(w_uq, 1, 2).astype(BF16),
        gq=q_norm_g[:, :, None],
        kv_lat_g=kv_lat_g[:, None, :],
        w_uk=w_ukv4[..., :QK_NOPE_DIM].reshape(depth, rkv, heads * QK_NOPE_DIM).astype(BF16),
        w_uvt=jnp.swapaxes(w_ukv4[..., QK_NOPE_DIM:].reshape(depth, rkv, heads * V_HEAD_DIM), 1, 2).astype(BF16),
        gk_n=k_norm_g[:, None, :QK_NOPE_DIM],
        gk_r=jnp.pad(g_rope, gpad),
        gk_rs=jnp.pad(g_rope[:, :, perm], gpad),
        w_f=w_fourier.astype(BF16),
        w_o=w_o.astype(BF16),
    )
    ffn1 = tuple(w.astype(BF16) for w in (ffn1_w_gate, ffn1_w_up, ffn1_w_down))
    ffn2 = tuple(w.astype(BF16) for w in (ffn2_w_gate, ffn2_w_up, ffn2_w_down))

    cond = jnp.concatenate([c, c_ctx[None, :], jnp.zeros((COND_ROWS - batch - 1, d), c.dtype)], axis=0)
    mods_all = _adaln(cond, w_ada, b_ada).reshape(depth, COND_ROWS, N_MOD, d)

    xa = jnp.concatenate([x.reshape(lat_rows, d), ctx.reshape(ctx_rows, d)], axis=0)
    dims = (heads, rq, rkv, fw, groups)
    for layer in range(depth):
        last = layer == depth - 1
        mods = mods_all[layer]
        g = norm_g[layer]
        xa = _ffn(xa, mods, g[0:1], *ffn1, layer, mod0=0, rows=rows, seq=seq)
        qt, k, vt, zr, zi = _mixer_in(xa, mods, g[1:2], p, layer, seq=seq, lat_rows=lat_rows, dims=dims)
        z2 = _dft_a(zr, zi, p, batch=batch, n1=n1, n2=n2, fw=fw)
        y = _dft_b(z2, p, rows=rows, batch=batch, n1=n1, n2=n2, fw=fw)
        o = _attention(qt, k, vt, batch=batch, seq=seq, ctx_len=ctx_len, heads=heads)
        if not last:
            o, y = _ctx_mixer(qt, k, vt, zr, zi, o, y, p, batch=batch, lat_rows=lat_rows, ctx_len=ctx_len,
                              heads=heads, fw=fw)
        out_rows = lat_rows if last else rows
        xa = _mixer_out(xa, mods, o, y, p, layer, rows=out_rows, seq=seq, groups=groups)
        xa = _ffn(xa, mods, g[2:3], *ffn2, layer, mod0=6, rows=out_rows, seq=seq)
    return xa.reshape(batch, seq, d)
```

```python
import functools
import math

import jax
import jax.numpy as jnp
import numpy as np
from jax import lax
from jax.experimental import pallas as pl
from jax.experimental.pallas import tpu as pltpu

GRID_W = 64
V_HEAD_DIM = 128
QK_NOPE_DIM = 128
QK_ROPE_DIM = 64
QK_DIM = QK_NOPE_DIM + QK_ROPE_DIM
FOURIER_GROUPS = 4
N_MOD = 9
AXIS_FREQS = QK_ROPE_DIM // 4
ROPE_THETA = 10000.0
SM_SCALE = QK_DIM ** -0.5
RMS_EPS = 1e-6
LOG2E = 1.4426950408889634

LANES = 128
MXU_COLS = 256
COND_ROWS = 8
VMEM_LIMIT = 56 * 1024 * 1024

ROW_TILE = 512
KEY_CHUNK = 256
FFN_TILE = 512
ADA_TILE = 1024
ATT_TQ = 1024
ATT_TK = 512
ATT_UNROLL = 16

BF16 = jnp.bfloat16
F32 = jnp.float32


def _dot(a, b):
    return jnp.dot(a, b, preferred_element_type=F32)


def _dot_nt(a, b):
    return lax.dot_general(a, b, (((1,), (1,)), ((), ())), preferred_element_type=F32)


def _cparams(sem):
    return pltpu.CompilerParams(dimension_semantics=sem, vmem_limit_bytes=VMEM_LIMIT)


def _norm_mod(x, g, shift, scale):
    y = x * lax.rsqrt(jnp.mean(x * x, axis=-1, keepdims=True) + RMS_EPS) * g
    return y * (1.0 + scale) + shift


def _adaln_kernel(c_ref, w_ref, b_ref, o_ref):
    c = c_ref[...]
    s = (c * jax.nn.sigmoid(c)).astype(BF16)
    o_ref[...] = _dot(s, w_ref[...].astype(BF16)) + b_ref[...]


def _adaln(cond, w_ada, b_ada):
    depth, d, n = w_ada.shape
    tn = math.gcd(ADA_TILE, d)
    return pl.pallas_call(
        _adaln_kernel,
        grid=(depth, n // tn),
        in_specs=[
            pl.BlockSpec((COND_ROWS, d), lambda l, j: (0, 0)),
            pl.BlockSpec((None, d, tn), lambda l, j: (l, 0, j)),
            pl.BlockSpec((None, 1, tn), lambda l, j: (l, 0, j)),
        ],
        out_specs=pl.BlockSpec((None, COND_ROWS, tn), lambda l, j: (l, 0, j)),
        out_shape=jax.ShapeDtypeStruct((depth, COND_ROWS, n), F32),
        compiler_params=_cparams(("arbitrary", "arbitrary")),
        name="adaln",
    )(cond, w_ada, b_ada.reshape(depth, 1, n))


def _ffn_kernel(x_ref, mod_ref, g_ref, wg_ref, wu_ref, wd_ref, o_ref, h_ref, acc_ref, *, mod0):
    f = pl.program_id(1)

    @pl.when(f == 0)
    def _():
        h = _norm_mod(x_ref[...], g_ref[...], mod_ref[mod0:mod0 + 1, :], mod_ref[mod0 + 1:mod0 + 2, :])
        h_ref[...] = h.astype(BF16)
        acc_ref[...] = jnp.zeros_like(acc_ref)

    h = h_ref[...]
    gate = _dot(h, wg_ref[...])
    up = _dot(h, wu_ref[...])
    a = (gate * jax.nn.sigmoid(gate) * up).astype(BF16)
    acc_ref[...] += _dot(a, wd_ref[...])

    @pl.when(f == pl.num_programs(1) - 1)
    def _():
        o_ref[...] = x_ref[...] + (0.5 * mod_ref[mod0 + 2:mod0 + 3, :]) * acc_ref[...]


def _ffn(xa, mods, g, wg, wu, wd, layer, *, mod0, rows, seq):
    d = xa.shape[1]
    d_ff = wg.shape[2]
    tm, tf = ROW_TILE, FFN_TILE
    row = lambda i, f: (i, 0)
    return pl.pallas_call(
        functools.partial(_ffn_kernel, mod0=mod0),
        grid=(rows // tm, d_ff // tf),
        in_specs=[
            pl.BlockSpec((tm, d), row),
            pl.BlockSpec((None, N_MOD, d), lambda i, f: ((i * tm) // seq, 0, 0)),
            pl.BlockSpec((1, d), lambda i, f: (0, 0)),
            pl.BlockSpec((None, d, tf), lambda i, f: (layer, 0, f)),
            pl.BlockSpec((None, d, tf), lambda i, f: (layer, 0, f)),
            pl.BlockSpec((None, tf, d), lambda i, f: (layer, f, 0)),
        ],
        out_specs=pl.BlockSpec((tm, d), row),
        out_shape=jax.ShapeDtypeStruct((rows, d), F32),
        scratch_shapes=[pltpu.VMEM((tm, d), BF16), pltpu.VMEM((tm, d), F32)],
        compiler_params=_cparams(("parallel", "arbitrary")),
        name="ffn",
    )(xa, mods, g, wg, wu, wd)


def _mixer_in_kernel(x_ref, mod_ref, g_ref, win_ref, gql_ref, wuqt_ref, gq_ref, gkvl_ref, wuk_ref, wuvt_ref,
                     gkn_ref, gkr_ref, gkrs_ref, ck_ref, sk_ref, cq_ref, sq_ref, dft_ref,
                     qt_ref, k_ref, vt_ref, zr_ref, zi_ref, *, heads, rq, rkv, fw, groups):
    tm = x_ref.shape[0]
    h = _norm_mod(x_ref[...], g_ref[...], mod_ref[3:4, :], mod_ref[4:5, :]).astype(BF16)
    proj = _dot(h, win_ref[...])
    f0 = rq + rkv
    r0 = f0 + fw

    qlat = proj[:, :rq]
    qn = (qlat * lax.rsqrt(jnp.mean(qlat * qlat, axis=-1, keepdims=True) + RMS_EPS) * gql_ref[...]).astype(BF16)
    qt = _dot_nt(wuqt_ref[...], qn)
    gq = gq_ref[...]
    cq = cq_ref[...]
    sq = sq_ref[...]
    a = AXIS_FREQS
    for hd in range(heads):
        blk = qt[hd * QK_DIM:(hd + 1) * QK_DIM]
        r = lax.rsqrt(jnp.mean(blk * blk, axis=0, keepdims=True) + RMS_EPS) * (SM_SCALE * LOG2E)
        blk = blk * r * gq
        rp = blk[QK_NOPE_DIM:]
        sw = jnp.concatenate([rp[a:2 * a], rp[:a], rp[3 * a:], rp[2 * a:3 * a]], axis=0)
        qt_ref[hd, :QK_NOPE_DIM, :] = blk[:QK_NOPE_DIM].astype(BF16)
        qt_ref[hd, QK_NOPE_DIM:, :] = (rp * cq + sw * sq).astype(BF16)

    kvlat = proj[:, rq:f0]
    kvn = (kvlat * lax.rsqrt(jnp.mean(kvlat * kvlat, axis=-1, keepdims=True) + RMS_EPS) * gkvl_ref[...]).astype(BF16)
    knope = _dot(kvn, wuk_ref[...])
    vt = _dot_nt(wuvt_ref[...], kvn)
    for c in range(tm // KEY_CHUNK):
        piece = vt[:, c * KEY_CHUNK:(c + 1) * KEY_CHUNK].astype(BF16)
        vt_ref[:, c] = piece.reshape(heads, V_HEAD_DIM, KEY_CHUNK)
    kr = proj[:, r0:r0 + LANES]
    krs = proj[:, r0 + LANES:r0 + 2 * LANES]
    krr = kr * gkr_ref[...] * ck_ref[...] + krs * gkrs_ref[...] * sk_ref[...]
    ss_r = jnp.sum(kr * kr, axis=-1, keepdims=True)
    gkn = gkn_ref[...]
    for hd in range(heads):
        kn = knope[:, hd * QK_NOPE_DIM:(hd + 1) * QK_NOPE_DIM]
        ss = jnp.sum(kn * kn, axis=-1, keepdims=True) + ss_r
        r = lax.rsqrt(ss * (1.0 / QK_DIM) + RMS_EPS)
        k_ref[hd, :, :QK_NOPE_DIM] = (kn * r * gkn).astype(BF16)
        k_ref[hd, :, QK_NOPE_DIM:] = (krr * r)[:, :QK_ROPE_DIM].astype(BF16)

    ch = fw // groups
    fb = proj[:, f0:r0].astype(BF16)
    dft = dft_ref[...]
    for gi in range(groups):
        uw = _dot(fb[:, gi * ch:(gi + 1) * ch], dft)
        zr_ref[:, gi * ch:(gi + 1) * ch] = uw[:, :ch].astype(BF16)
        zi_ref[:, gi * ch:(gi + 1) * ch] = uw[:, ch:].astype(BF16)


def _mixer_in(xa, mods, g, p, layer, *, seq, lat_rows, dims):
    rows, d = xa.shape
    heads, rq, rkv, fw, groups = dims
    tm = ROW_TILE
    ncols = p["w_in"].shape[2]
    n_lat = lat_rows // tm
    per_seq = seq // tm
    const2 = lambda i: (0, 0)
    row = lambda i: (i, 0)
    pos = lambda i: (jnp.where(i < n_lat, i % per_seq, per_seq), 0)
    post = lambda i: (0, jnp.where(i < n_lat, i % per_seq, per_seq))
    lay = lambda *shape: pl.BlockSpec((None,) + shape, lambda i: (layer,) + (0,) * len(shape),
                                      pipeline_mode=pl.Buffered(1))
    ch = fw // groups
    out_shape = [
        jax.ShapeDtypeStruct((heads, QK_DIM, rows), BF16),
        jax.ShapeDtypeStruct((heads, rows, QK_DIM), BF16),
        jax.ShapeDtypeStruct((heads, rows // KEY_CHUNK, V_HEAD_DIM, KEY_CHUNK), BF16),
        jax.ShapeDtypeStruct((rows, fw), BF16),
        jax.ShapeDtypeStruct((rows, fw), BF16),
    ]
    out_specs = [
        pl.BlockSpec((heads, QK_DIM, tm), lambda i: (0, 0, i)),
        pl.BlockSpec((heads, tm, QK_DIM), lambda i: (0, i, 0)),
        pl.BlockSpec((heads, tm // KEY_CHUNK, V_HEAD_DIM, KEY_CHUNK), lambda i: (0, i, 0, 0)),
        pl.BlockSpec((tm, fw), row),
        pl.BlockSpec((tm, fw), row),
    ]
    in_specs = [
        pl.BlockSpec((tm, d), row),
        pl.BlockSpec((None, N_MOD, d), lambda i: ((i * tm) // seq, 0, 0)),
        pl.BlockSpec((1, d), const2),
        lay(d, ncols),
        lay(1, rq),
        lay(heads * QK_DIM, rq),
        lay(QK_DIM, 1),
        lay(1, rkv),
        lay(rkv, heads * QK_NOPE_DIM),
        lay(heads * V_HEAD_DIM, rkv),
        lay(1, LANES), lay(1, LANES), lay(1, LANES),
        pl.BlockSpec((tm, LANES), pos), pl.BlockSpec((tm, LANES), pos),
        pl.BlockSpec((QK_ROPE_DIM, tm), post), pl.BlockSpec((QK_ROPE_DIM, tm), post),
        pl.BlockSpec((ch, 2 * ch), const2),
    ]
    return pl.pallas_call(
        functools.partial(_mixer_in_kernel, heads=heads, rq=rq, rkv=rkv, fw=fw, groups=groups),
        grid=(rows // tm,),
        in_specs=in_specs,
        out_specs=out_specs,
        out_shape=out_shape,
        compiler_params=_cparams(("parallel",)),
        name="mixer_in",
    )(xa, mods, g, p["w_in"], p["q_lat_g"], p["w_uqt"], p["gq"], p["kv_lat_g"], p["w_uk"], p["w_uvt"],
      p["gk_n"], p["gk_r"], p["gk_rs"], p["rope_ck"], p["rope_sk"], p["rope_cqt"], p["rope_sqt"], p["dft_ch"])


DFT_BATCH = 16


def _dft_a_kernel(zr_ref, zi_ref, fa_ref, tc_ref, ts_ref, o_ref, *, fw):
    n1 = fa_ref.shape[0] // 2
    fa = fa_ref[...]
    reps = fw // LANES
    for t in range(DFT_BATCH):
        cols = slice(t * fw, (t + 1) * fw)
        zst = jnp.concatenate([zr_ref[:, cols], zi_ref[:, cols]], axis=0)
        z1 = _dot(fa, zst)
        z1r, z1i = z1[:n1], z1[n1:]
        tc = jnp.concatenate([tc_ref[t]] * reps, axis=1)
        ts = jnp.concatenate([ts_ref[t]] * reps, axis=1)
        o_ref[0, :, t, :] = (z1r * tc + z1i * ts).astype(BF16)
        o_ref[1, :, t, :] = (z1i * tc - z1r * ts).astype(BF16)


def _dft_a(zr, zi, p, *, batch, n1, n2, fw):
    rows = zr.shape[0]
    cb = DFT_BATCH
    zr2 = zr.reshape(rows // n2, n2 * fw)
    zi2 = zi.reshape(rows // n2, n2 * fw)
    blk = pl.BlockSpec((n1, cb * fw), lambda b, j: (b, j))
    tw = pl.BlockSpec((cb, n1, LANES), lambda b, j: (j, 0, 0))
    return pl.pallas_call(
        functools.partial(_dft_a_kernel, fw=fw),
        grid=(batch, n2 // cb),
        in_specs=[blk, blk, pl.BlockSpec((2 * n1, 2 * n1), lambda b, j: (0, 0)), tw, tw],
        out_specs=pl.BlockSpec((None, 2, n1, cb, fw), lambda b, j: (b, 0, 0, j, 0)),
        out_shape=jax.ShapeDtypeStruct((batch, 2, n1, n2, fw), BF16),
        compiler_params=_cparams(("parallel", "parallel")),
        name="dft_rows_a",
    )(zr2, zi2, p["dft_a"], p["tw_c"], p["tw_s"])


def _dft_b_kernel(z_ref, fb_ref, o_ref):
    fb = fb_ref[...]
    n2, fw = z_ref.shape[2], z_ref.shape[3]
    for t in range(DFT_BATCH):
        o_ref[:, t, :] = _dot(fb, z_ref[:, t].reshape(2 * n2, fw)).astype(BF16)


def _dft_b(z2, p, *, rows, batch, n1, n2, fw):
    cb = DFT_BATCH
    y = pl.pallas_call(
        _dft_b_kernel,
        grid=(batch, n1 // cb),
        in_specs=[
            pl.BlockSpec((None, 2, cb, n2, fw), lambda b, j: (b, 0, j, 0, 0)),
            pl.BlockSpec((n2, 2 * n2), lambda b, j: (0, 0)),
        ],
        out_specs=pl.BlockSpec((n2, cb, fw), lambda b, j: (b, j, 0)),
        out_shape=jax.ShapeDtypeStruct((rows // n1, n1, fw), BF16),
        compiler_params=_cparams(("parallel", "parallel")),
        name="dft_rows_b",
    )(z2, p["dft_b"])
    return y.reshape(rows, fw)


def _attn_kernel(qt_ref, k_ref, kc_ref, vt_ref, vtc_ref, o_ref, s_ref, acc_ref, *, tk, unroll):
    qt = qt_ref[...]
    tq = qt.shape[1]
    per = tk // KEY_CHUNK
    n = k_ref.shape[0] // tk

    def scores(j, slot):
        s = _dot(k_ref[pl.ds(pl.multiple_of(j * tk, tk), tk), :], qt)
        s_ref[slot] = s
        return jnp.max(s, axis=0, keepdims=True)

    def absorb(load_s, vts, cmax, m, l):
        m_new = jnp.maximum(m, cmax)
        alpha = jnp.exp2(m - m_new)
        vt = vts[0] if len(vts) == 1 else jnp.concatenate(vts, axis=1)
        sums = []
        for c in range(0, tq, MXU_COLS):
            cs = slice(c, c + MXU_COLS)
            p = jnp.exp2(load_s(cs) - m_new[:, cs])
            sums.append(jnp.sum(p, axis=0, keepdims=True))
            acc_ref[:, cs] = alpha[:, cs] * acc_ref[:, cs] + _dot(vt, p.astype(BF16))
        return m_new, alpha * l + jnp.concatenate(sums, axis=1)

    def absorb_chunk(j, slot, cmax, m, l):
        return absorb(lambda cs: s_ref[slot, :, cs], [vt_ref[j * per + c] for c in range(per)], cmax, m, l)

    acc_ref[...] = jnp.zeros_like(acc_ref)
    sc = _dot(kc_ref[...], qt)
    cmax0 = scores(0, 0)
    m, l = absorb(lambda cs: sc[:, cs], [vtc_ref[0]], jnp.max(sc, axis=0, keepdims=True),
                  jnp.full((1, tq), -1e30, F32), jnp.zeros((1, tq), F32))

    def group(i, carry):
        m, l, cmax = carry
        for u in range(unroll):
            j = i * unroll + u
            cmax_next = scores(jnp.minimum(j + 1, n - 1), (u + 1) % 2)
            m, l = absorb_chunk(j, u % 2, cmax, m, l)
            cmax = cmax_next
        return m, l, cmax

    m, l, _ = lax.fori_loop(0, n // unroll, group, (m, l, cmax0))
    o_ref[...] = (acc_ref[...] / l).T.astype(BF16)


def _attention(qt, k, vt, *, batch, seq, ctx_len, heads):
    rows = k.shape[1]
    tq, tk = ATT_TQ, ATT_TK
    nq = seq // tq
    lat_rows = batch * seq
    return pl.pallas_call(
        functools.partial(_attn_kernel, tk=tk, unroll=math.gcd(ATT_UNROLL, seq // tk)),
        grid=(batch, heads, nq),
        in_specs=[
            pl.BlockSpec((None, QK_DIM, tq), lambda b, h, i: (h, 0, b * nq + i)),
            pl.BlockSpec((None, seq, QK_DIM), lambda b, h, i: (h, b, 0)),
            pl.BlockSpec((None, ctx_len, QK_DIM), lambda b, h, i: (h, lat_rows // ctx_len + b, 0)),
            pl.BlockSpec((None, seq // KEY_CHUNK, V_HEAD_DIM, KEY_CHUNK), lambda b, h, i: (h, b, 0, 0)),
            pl.BlockSpec((None, 1, V_HEAD_DIM, KEY_CHUNK), lambda b, h, i: (h, lat_rows // KEY_CHUNK + b, 0, 0)),
        ],
        out_specs=pl.BlockSpec((tq, V_HEAD_DIM), lambda b, h, i: (b * nq + i, h)),
        out_shape=jax.ShapeDtypeStruct((rows, heads * V_HEAD_DIM), BF16),
        scratch_shapes=[pltpu.VMEM((2, tk, tq), F32), pltpu.VMEM((V_HEAD_DIM, tq), F32)],
        compiler_params=_cparams(("parallel", "parallel", "arbitrary")),
        name="attention",
    )(qt, k, k, vt, vt)


def _ctx_kernel(qt_ref, k_ref, vt_ref, zr_ref, zi_ref, fc_ref, o_in, y_in, o_ref, y_ref, *, heads):
    del o_in, y_in
    for hd in range(heads):
        s = _dot(k_ref[hd], qt_ref[hd])
        p = jnp.exp2(s - jnp.max(s, axis=0, keepdims=True))
        l = jnp.sum(p, axis=0, keepdims=True)
        o = _dot(vt_ref[hd, 0], p.astype(BF16)) / l
        o_ref[:, hd * V_HEAD_DIM:(hd + 1) * V_HEAD_DIM] = o.T.astype(BF16)
    zst = jnp.concatenate([zr_ref[...], zi_ref[...]], axis=0)
    y_ref[...] = _dot(fc_ref[...], zst).astype(BF16)


def _ctx_mixer(qt, k, vt, zr, zi, o, y, p, *, batch, lat_rows, ctx_len, heads, fw):
    c0 = lat_rows // ctx_len
    return pl.pallas_call(
        functools.partial(_ctx_kernel, heads=heads),
        grid=(batch,),
        in_specs=[
            pl.BlockSpec((heads, QK_DIM, ctx_len), lambda b: (0, 0, c0 + b)),
            pl.BlockSpec((heads, ctx_len, QK_DIM), lambda b: (0, c0 + b, 0)),
            pl.BlockSpec((heads, 1, V_HEAD_DIM, KEY_CHUNK), lambda b: (0, c0 + b, 0, 0)),
            pl.BlockSpec((ctx_len, fw), lambda b: (c0 + b, 0)),
            pl.BlockSpec((ctx_len, fw), lambda b: (c0 + b, 0)),
            pl.BlockSpec((ctx_len, 2 * ctx_len), lambda b: (0, 0)),
            pl.BlockSpec(memory_space=pl.ANY),
            pl.BlockSpec(memory_space=pl.ANY),
        ],
        out_specs=[
            pl.BlockSpec((ctx_len, heads * V_HEAD_DIM), lambda b: (c0 + b, 0)),
            pl.BlockSpec((ctx_len, fw), lambda b: (c0 + b, 0)),
        ],
        out_shape=[jax.ShapeDtypeStruct(o.shape, o.dtype), jax.ShapeDtypeStruct(y.shape, y.dtype)],
        input_output_aliases={6: 0, 7: 1},
        compiler_params=_cparams(("parallel",)),
        name="ctx_mixer",
    )(qt, k, vt, zr, zi, p["dft_ctx"], o, y)


def _mixer_out_kernel(x_ref, mod_ref, a_ref, y_ref, wf_ref, wo_ref, o_ref, *, groups):
    ch = y_ref.shape[1] // groups
    parts = [a_ref[...]]
    for gi in range(groups):
        parts.append(_dot(y_ref[:, gi * ch:(gi + 1) * ch], wf_ref[gi]).astype(BF16))
    cat = jnp.concatenate(parts, axis=1)
    o_ref[...] = x_ref[...] + mod_ref[5:6, :] * _dot(cat, wo_ref[...])


def _mixer_out(xa, mods, o, y, p, layer, *, rows, seq, groups):
    d = xa.shape[1]
    tm = ROW_TILE
    aw, fw = o.shape[1], y.shape[1]
    ch = fw // groups
    row = lambda i: (i, 0)
    return pl.pallas_call(
        functools.partial(_mixer_out_kernel, groups=groups),
        grid=(rows // tm,),
        in_specs=[
            pl.BlockSpec((tm, d), row),
            pl.BlockSpec((None, N_MOD, d), lambda i: ((i * tm) // seq, 0, 0)),
            pl.BlockSpec((tm, aw), row),
            pl.BlockSpec((tm, fw), row),
            pl.BlockSpec((None, groups, ch, ch), lambda i: (layer, 0, 0, 0), pipeline_mode=pl.Buffered(1)),
            pl.BlockSpec((None, aw + fw, d), lambda i: (layer, 0, 0), pipeline_mode=pl.Buffered(1)),
        ],
        out_specs=pl.BlockSpec((tm, d), row),
        out_shape=jax.ShapeDtypeStruct((rows, d), F32),
        compiler_params=_cparams(("parallel",)),
        name="mixer_out",
    )(xa, mods, o, y, p["w_f"], p["w_o"])


def _dft_parts(n):
    idx = np.arange(n, dtype=np.int64)
    ang = 2.0 * np.pi * ((idx[:, None] * idx[None, :]) % n).astype(np.float64) / n
    s = 1.0 / math.sqrt(n)
    return np.cos(ang) * s, np.sin(ang) * s


def _tables(seq, ctx_len, ch, n1, n2, tm):
    t = {}
    c, s = _dft_parts(ch)
    t["dft_ch"] = jnp.asarray(np.concatenate([c, -s], axis=1), BF16)
    c, s = _dft_parts(n1)
    t["dft_a"] = jnp.asarray(np.block([[c, s], [-s, c]]), BF16)
    c, s = _dft_parts(n2)
    t["dft_b"] = jnp.asarray(np.concatenate([c, s], axis=1), BF16)
    c, s = _dft_parts(ctx_len)
    t["dft_ctx"] = jnp.asarray(np.concatenate([c, s], axis=1), BF16)
    i2 = np.arange(n2, dtype=np.int64)[:, None]
    k1 = np.arange(n1, dtype=np.int64)[None, :]
    ang = 2.0 * np.pi * ((i2 * k1) % seq).astype(np.float64) / seq
    t["tw_c"] = jnp.asarray(np.broadcast_to(np.cos(ang)[:, :, None], (n2, n1, LANES)), F32)
    t["tw_s"] = jnp.asarray(np.broadcast_to(np.sin(ang)[:, :, None], (n2, n1, LANES)), F32)

    rows = seq // GRID_W
    row = jnp.broadcast_to(jnp.arange(rows, dtype=F32)[:, None], (rows, GRID_W)).reshape(-1)
    col = jnp.broadcast_to(jnp.arange(GRID_W, dtype=F32)[None, :], (rows, GRID_W)).reshape(-1)
    inv_freq = ROPE_THETA ** (-jnp.arange(AXIS_FREQS, dtype=F32) / AXIS_FREQS)
    ang_r = row[:, None] * inv_freq
    ang_c = col[:, None] * inv_freq
    cr, sr, cc, sc = jnp.cos(ang_r), jnp.sin(ang_r), jnp.cos(ang_c), jnp.sin(ang_c)
    cos = jnp.concatenate([cr, cr, cc, cc], axis=1)
    sin = jnp.concatenate([-sr, sr, -sc, sc], axis=1)
    cos = jnp.concatenate([cos, jnp.ones((tm, QK_ROPE_DIM), F32)], axis=0)
    sin = jnp.concatenate([sin, jnp.zeros((tm, QK_ROPE_DIM), F32)], axis=0)
    pad = ((0, 0), (0, LANES - QK_ROPE_DIM))
    t["rope_ck"] = jnp.pad(cos, pad)
    t["rope_sk"] = jnp.pad(sin, pad)
    t["rope_cqt"] = cos.T
    t["rope_sqt"] = sin.T
    return t


def _swap_perm():
    a = AXIS_FREQS
    return np.concatenate([np.arange(a, 2 * a), np.arange(0, a), np.arange(3 * a, 4 * a), np.arange(2 * a, 3 * a)])


def kernel(x, c, ctx, c_ctx, w_ada, b_ada, norm_g, ffn1_w_gate, ffn1_w_up, ffn1_w_down, ffn2_w_gate, ffn2_w_up,
           ffn2_w_down, w_in, q_lat_g, w_uq, kv_lat_g, w_ukv, q_norm_g, k_norm_g, w_fourier, w_o):
    batch, seq, d = x.shape
    ctx_len = ctx.shape[1]
    depth = w_ada.shape[0]
    rq = q_lat_g.shape[1]
    rkv = kv_lat_g.shape[1]
    heads = w_uq.shape[2] // QK_DIM
    groups = w_fourier.shape[1]
    ch = w_fourier.shape[2]
    fw = groups * ch
    tm = ROW_TILE
    lat_rows = batch * seq
    ctx_rows = batch * ctx_len
    rows = lat_rows + ctx_rows
    n1 = 1 << ((seq.bit_length() - 1) // 2)
    n2 = seq // n1
    assert n1 * n2 == seq and n1 % DFT_BATCH == 0 and n2 % DFT_BATCH == 0
    assert ctx_len == KEY_CHUNK and seq % tm == 0 and ctx_rows % tm == 0 and ctx_rows <= seq
    assert rows % n1 == 0 and rows % n2 == 0 and seq % ATT_TQ == 0 and seq % (2 * ATT_TK) == 0
    assert batch + 1 <= COND_ROWS and w_in.shape[2] == rq + rkv + QK_ROPE_DIM + fw
    assert w_ukv.shape[2] == heads * (QK_NOPE_DIM + V_HEAD_DIM)

    perm = _swap_perm()
    k0 = rq + rkv
    zpad = jnp.zeros((depth, d, LANES - QK_ROPE_DIM), w_in.dtype)
    w_rope = w_in[:, :, k0:k0 + QK_ROPE_DIM]
    w_in_ext = jnp.concatenate(
        [w_in[:, :, :k0], w_in[:, :, k0 + QK_ROPE_DIM:], w_rope, zpad, w_rope[:, :, perm], zpad], axis=2)
    w_ukv4 = w_ukv.reshape(depth, rkv, heads, QK_NOPE_DIM + V_HEAD_DIM)
    gpad = ((0, 0), (0, 0), (0, LANES - QK_ROPE_DIM))
    g_rope = k_norm_g[:, None, QK_NOPE_DIM:]
    p = _tables(seq, ctx_len, ch, n1, n2, tm)
    p.update(
        w_in=w_in_ext.astype(BF16),
        q_lat_g=q_lat_g[:, None, :],
        w_uqt=jnp.swapaxes(w_uq, 1, 2).astype(BF16),
        gq=q_norm_g[:, :, None],
        kv_lat_g=kv_lat_g[:, None, :],
        w_uk=w_ukv4[..., :QK_NOPE_DIM].reshape(depth, rkv, heads * QK_NOPE_DIM).astype(BF16),
        w_uvt=jnp.swapaxes(w_ukv4[..., QK_NOPE_DIM:].reshape(depth, rkv, heads * V_HEAD_DIM), 1, 2).astype(BF16),
        gk_n=k_norm_g[:, None, :QK_NOPE_DIM],
        gk_r=jnp.pad(g_rope, gpad),
        gk_rs=jnp.pad(g_rope[:, :, perm], gpad),
        w_f=w_fourier.astype(BF16),
        w_o=w_o.astype(BF16),
    )
    ffn1 = tuple(w.astype(BF16) for w in (ffn1_w_gate, ffn1_w_up, ffn1_w_down))
    ffn2 = tuple(w.astype(BF16) for w in (ffn2_w_gate, ffn2_w_up, ffn2_w_down))

    cond = jnp.concatenate([c, c_ctx[None, :], jnp.zeros((COND_ROWS - batch - 1, d), c.dtype)], axis=0)
    mods_all = _adaln(cond, w_ada, b_ada).reshape(depth, COND_ROWS, N_MOD, d)

    xa = jnp.concatenate([x.reshape(lat_rows, d), ctx.reshape(ctx_rows, d)], axis=0)
    dims = (heads, rq, rkv, fw, groups)
    for layer in range(depth):
        last = layer == depth - 1
        mods = mods_all[layer]
        g = norm_g[layer]
        xa = _ffn(xa, mods, g[0:1], *ffn1, layer, mod0=0, rows=rows, seq=seq)
        qt, k, vt, zr, zi = _mixer_in(xa, mods, g[1:2], p, layer, seq=seq, lat_rows=lat_rows, dims=dims)
        z2 = _dft_a(zr, zi, p, batch=batch, n1=n1, n2=n2, fw=fw)
        y = _dft_b(z2, p, rows=rows, batch=batch, n1=n1, n2=n2, fw=fw)
        o = _attention(qt, k, vt, batch=batch, seq=seq, ctx_len=ctx_len, heads=heads)
        if not last:
            o, y = _ctx_mixer(qt, k, vt, zr, zi, o, y, p, batch=batch, lat_rows=lat_rows, ctx_len=ctx_len,
                              heads=heads, fw=fw)
        out_rows = lat_rows if last else rows
        xa = _mixer_out(xa, mods, o, y, p, layer, rows=out_rows, seq=seq, groups=groups)
        xa = _ffn(xa, mods, g[2:3], *ffn2, layer, mod0=6, rows=out_rows, seq=seq)
    return xa.reshape(batch, seq, d)
```

```python
import functools
import math

import jax
import jax.numpy as jnp
import numpy as np
from jax import lax
from jax.experimental import pallas as pl
from jax.experimental.pallas import tpu as pltpu

GRID_W = 64
V_HEAD_DIM = 128
QK_NOPE_DIM = 128
QK_ROPE_DIM = 64
QK_DIM = QK_NOPE_DIM + QK_ROPE_DIM
FOURIER_GROUPS = 4
N_MOD = 9
AXIS_FREQS = QK_ROPE_DIM // 4
ROPE_THETA = 10000.0
SM_SCALE = QK_DIM ** -0.5
RMS_EPS = 1e-6
LOG2E = 1.4426950408889634

LANES = 128
MXU_COLS = 256
COND_ROWS = 8
VMEM_LIMIT = 60 * 1024 * 1024

ROW_TILE = 512
KEY_CHUNK = 256
FFN_TILE = 1024
ADA_TILE = 1024
ATT_TQ = 1024
ATT_TK = 512
ATT_UNROLL = 16

BF16 = jnp.bfloat16
F32 = jnp.float32


def _dot(a, b):
    return jnp.dot(a, b, preferred_element_type=F32)


def _dot_nt(a, b):
    return lax.dot_general(a, b, (((1,), (1,)), ((), ())), preferred_element_type=F32)


def _cparams(sem):
    return pltpu.CompilerParams(dimension_semantics=sem, vmem_limit_bytes=VMEM_LIMIT)


def _norm_mod(x, g, shift, scale):
    y = x * lax.rsqrt(jnp.mean(x * x, axis=-1, keepdims=True) + RMS_EPS) * g
    return y * (1.0 + scale) + shift


def _adaln_kernel(c_ref, w_ref, b_ref, o_ref):
    c = c_ref[...]
    s = (c * jax.nn.sigmoid(c)).astype(BF16)
    o_ref[...] = _dot(s, w_ref[...].astype(BF16)) + b_ref[...]


def _adaln(cond, w_ada, b_ada):
    depth, d, n = w_ada.shape
    tn = math.gcd(ADA_TILE, d)
    return pl.pallas_call(
        _adaln_kernel,
        grid=(depth, n // tn),
        in_specs=[
            pl.BlockSpec((COND_ROWS, d), lambda l, j: (0, 0)),
            pl.BlockSpec((None, d, tn), lambda l, j: (l, 0, j)),
            pl.BlockSpec((None, 1, tn), lambda l, j: (l, 0, j)),
        ],
        out_specs=pl.BlockSpec((None, COND_ROWS, tn), lambda l, j: (l, 0, j)),
        out_shape=jax.ShapeDtypeStruct((depth, COND_ROWS, n), F32),
        compiler_params=_cparams(("arbitrary", "arbitrary")),
        name="adaln",
    )(cond, w_ada, b_ada.reshape(depth, 1, n))


def _ffn_kernel(x_ref, mod_ref, g_ref, wg_ref, wu_ref, wd_ref, wgt_ref, wut_ref, wdt_ref, o_ref, h_ref, acc_ref,
                *, mod0, n_wide):
    f = pl.program_id(1)

    def block(wg, wu, wd):
        h = h_ref[...]
        gate = _dot(h, wg[...])
        up = _dot(h, wu[...])
        a = (gate * jax.nn.sigmoid(gate) * up).astype(BF16)
        return _dot(a, wd[...])

    @pl.when(f == 0)
    def _():
        h = _norm_mod(x_ref[...], g_ref[...], mod_ref[mod0:mod0 + 1, :], mod_ref[mod0 + 1:mod0 + 2, :])
        h_ref[...] = h.astype(BF16)
        acc_ref[...] = block(wgt_ref, wut_ref, wdt_ref)

    @pl.when((f > 0) & (f < n_wide))
    def _():
        acc_ref[...] += block(wg_ref, wu_ref, wd_ref)

    @pl.when(f == n_wide)
    def _():
        total = acc_ref[...] + block(wg_ref, wu_ref, wd_ref)
        o_ref[...] = x_ref[...] + (0.5 * mod_ref[mod0 + 2:mod0 + 3, :]) * total


def _ffn(xa, mods, g, wg, wu, wd, layer, *, mod0, rows, seq):
    d = xa.shape[1]
    d_ff = wg.shape[2]
    tm, tf = ROW_TILE, FFN_TILE
    n_wide = (d_ff - 1) // tf
    tail = d_ff - n_wide * tf
    assert n_wide >= 1 and tail % LANES == 0 and (n_wide * tf) % tail == 0
    tail_blk = (n_wide * tf) // tail
    row = lambda i, f: (i, 0)
    wide = lambda i, f: (layer, 0, jnp.maximum(f - 1, 0))
    wide_t = lambda i, f: (layer, jnp.maximum(f - 1, 0), 0)
    once = pl.Buffered(1)
    return pl.pallas_call(
        functools.partial(_ffn_kernel, mod0=mod0, n_wide=n_wide),
        grid=(rows // tm, n_wide + 1),
        in_specs=[
            pl.BlockSpec((tm, d), row),
            pl.BlockSpec((None, N_MOD, d), lambda i, f: ((i * tm) // seq, 0, 0)),
            pl.BlockSpec((1, d), lambda i, f: (0, 0)),
            pl.BlockSpec((None, d, tf), wide),
            pl.BlockSpec((None, d, tf), wide),
            pl.BlockSpec((None, tf, d), wide_t),
            pl.BlockSpec((None, d, tail), lambda i, f: (layer, 0, tail_blk), pipeline_mode=once),
            pl.BlockSpec((None, d, tail), lambda i, f: (layer, 0, tail_blk), pipeline_mode=once),
            pl.BlockSpec((None, tail, d), lambda i, f: (layer, tail_blk, 0), pipeline_mode=once),
        ],
        out_specs=pl.BlockSpec((tm, d), row),
        out_shape=jax.ShapeDtypeStruct((rows, d), F32),
        scratch_shapes=[pltpu.VMEM((tm, d), BF16), pltpu.VMEM((tm, d), F32)],
        compiler_params=_cparams(("parallel", "arbitrary")),
        name="ffn",
    )(xa, mods, g, wg, wu, wd, wg, wu, wd)


def _mixer_in_kernel(x_ref, mod_ref, g_ref, win_ref, gql_ref, wuqt_ref, gq_ref, gkvl_ref, wuk_ref, wuvt_ref,
                     gkn_ref, gkr_ref, gkrs_ref, ck_ref, sk_ref, cq_ref, sq_ref, dft_ref,
                     qt_ref, k_ref, vt_ref, zr_ref, zi_ref, *, heads, rq, rkv, fw, groups):
    tm = x_ref.shape[0]
    h = _norm_mod(x_ref[...], g_ref[...], mod_ref[3:4, :], mod_ref[4:5, :]).astype(BF16)
    proj = _dot(h, win_ref[...])
    f0 = rq + rkv
    r0 = f0 + fw

    qlat = proj[:, :rq]
    qn = (qlat * lax.rsqrt(jnp.mean(qlat * qlat, axis=-1, keepdims=True) + RMS_EPS) * gql_ref[...]).astype(BF16)
    qt = _dot_nt(wuqt_ref[...], qn)
    gq = gq_ref[...]
    cq = cq_ref[...]
    sq = sq_ref[...]
    a = AXIS_FREQS
    for hd in range(heads):
        blk = qt[hd * QK_DIM:(hd + 1) * QK_DIM]
        r = lax.rsqrt(jnp.mean(blk * blk, axis=0, keepdims=True) + RMS_EPS) * (SM_SCALE * LOG2E)
        blk = blk * r * gq
        rp = blk[QK_NOPE_DIM:]
        sw = jnp.concatenate([rp[a:2 * a], rp[:a], rp[3 * a:], rp[2 * a:3 * a]], axis=0)
        qt_ref[hd, :QK_NOPE_DIM, :] = blk[:QK_NOPE_DIM].astype(BF16)
        qt_ref[hd, QK_NOPE_DIM:, :] = (rp * cq + sw * sq).astype(BF16)

    kvlat = proj[:, rq:f0]
    kvn = (kvlat * lax.rsqrt(jnp.mean(kvlat * kvlat, axis=-1, keepdims=True) + RMS_EPS) * gkvl_ref[...]).astype(BF16)
    knope = _dot(kvn, wuk_ref[...])
    vt = _dot_nt(wuvt_ref[...], kvn)
    for c in range(tm // KEY_CHUNK):
        piece = vt[:, c * KEY_CHUNK:(c + 1) * KEY_CHUNK].astype(BF16)
        vt_ref[:, c] = piece.reshape(heads, V_HEAD_DIM, KEY_CHUNK)
    kr = proj[:, r0:r0 + LANES]
    krs = proj[:, r0 + LANES:r0 + 2 * LANES]
    krr = kr * gkr_ref[...] * ck_ref[...] + krs * gkrs_ref[...] * sk_ref[...]
    ss_r = jnp.sum(kr * kr, axis=-1, keepdims=True)
    gkn = gkn_ref[...]
    for hd in range(heads):
        kn = knope[:, hd * QK_NOPE_DIM:(hd + 1) * QK_NOPE_DIM]
        ss = jnp.sum(kn * kn, axis=-1, keepdims=True) + ss_r
        r = lax.rsqrt(ss * (1.0 / QK_DIM) + RMS_EPS)
        k_ref[hd, :, :QK_NOPE_DIM] = (kn * r * gkn).astype(BF16)
        k_ref[hd, :, QK_NOPE_DIM:] = (krr * r)[:, :QK_ROPE_DIM].astype(BF16)

    ch = fw // groups
    fb = proj[:, f0:r0].astype(BF16)
    dft = dft_ref[...]
    for gi in range(groups):
        uw = _dot(fb[:, gi * ch:(gi + 1) * ch], dft)
        zr_ref[:, gi * ch:(gi + 1) * ch] = uw[:, :ch].astype(BF16)
        zi_ref[:, gi * ch:(gi + 1) * ch] = uw[:, ch:].astype(BF16)


def _mixer_in(xa, mods, g, p, layer, *, seq, lat_rows, dims):
    rows, d = xa.shape
    heads, rq, rkv, fw, groups = dims
    tm = ROW_TILE
    ncols = p["w_in"].shape[2]
    n_lat = lat_rows // tm
    per_seq = seq // tm
    const2 = lambda i: (0, 0)
    row = lambda i: (i, 0)
    pos = lambda i: (jnp.where(i < n_lat, i % per_seq, per_seq), 0)
    post = lambda i: (0, jnp.where(i < n_lat, i % per_seq, per_seq))
    lay = lambda *shape: pl.BlockSpec((None,) + shape, lambda i: (layer,) + (0,) * len(shape),
                                      pipeline_mode=pl.Buffered(1))
    ch = fw // groups
    out_shape = [
        jax.ShapeDtypeStruct((heads, QK_DIM, rows), BF16),
        jax.ShapeDtypeStruct((heads, rows, QK_DIM), BF16),
        jax.ShapeDtypeStruct((heads, rows // KEY_CHUNK, V_HEAD_DIM, KEY_CHUNK), BF16),
        jax.ShapeDtypeStruct((rows, fw), BF16),
        jax.ShapeDtypeStruct((rows, fw), BF16),
    ]
    out_specs = [
        pl.BlockSpec((heads, QK_DIM, tm), lambda i: (0, 0, i)),
        pl.BlockSpec((heads, tm, QK_DIM), lambda i: (0, i, 0)),
        pl.BlockSpec((heads, tm // KEY_CHUNK, V_HEAD_DIM, KEY_CHUNK), lambda i: (0, i, 0, 0)),
        pl.BlockSpec((tm, fw), row),
        pl.BlockSpec((tm, fw), row),
    ]
    in_specs = [
        pl.BlockSpec((tm, d), row),
        pl.BlockSpec((None, N_MOD, d), lambda i: ((i * tm) // seq, 0, 0)),
        pl.BlockSpec((1, d), const2),
        lay(d, ncols),
        lay(1, rq),
        lay(heads * QK_DIM, rq),
        lay(QK_DIM, 1),
        lay(1, rkv),
        lay(rkv, heads * QK_NOPE_DIM),
        lay(heads * V_HEAD_DIM, rkv),
        lay(1, LANES), lay(1, LANES), lay(1, LANES),
        pl.BlockSpec((tm, LANES), pos), pl.BlockSpec((tm, LANES), pos),
        pl.BlockSpec((QK_ROPE_DIM, tm), post), pl.BlockSpec((QK_ROPE_DIM, tm), post),
        pl.BlockSpec((ch, 2 * ch), const2),
    ]
    return pl.pallas_call(
        functools.partial(_mixer_in_kernel, heads=heads, rq=rq, rkv=rkv, fw=fw, groups=groups),
        grid=(rows // tm,),
        in_specs=in_specs,
        out_specs=out_specs,
        out_shape=out_shape,
        compiler_params=_cparams(("parallel",)),
        name="mixer_in",
    )(xa, mods, g, p["w_in"], p["q_lat_g"], p["w_uqt"], p["gq"], p["kv_lat_g"], p["w_uk"], p["w_uvt"],
      p["gk_n"], p["gk_r"], p["gk_rs"], p["rope_ck"], p["rope_sk"], p["rope_cqt"], p["rope_sqt"], p["dft_ch"])


DFT_BATCH = 16


def _dft_a_kernel(zr_ref, zi_ref, fa_ref, tc_ref, ts_ref, o_ref, *, fw):
    n1 = fa_ref.shape[0] // 2
    fa = fa_ref[...]
    reps = fw // LANES
    for t in range(DFT_BATCH):
        cols = slice(t * fw, (t + 1) * fw)
        zst = jnp.concatenate([zr_ref[:, cols], zi_ref[:, cols]], axis=0)
        z1 = _dot(fa, zst)
        z1r, z1i = z1[:n1], z1[n1:]
        tc = jnp.concatenate([tc_ref[t]] * reps, axis=1)
        ts = jnp.concatenate([ts_ref[t]] * reps, axis=1)
        o_ref[0, :, t, :] = (z1r * tc + z1i * ts).astype(BF16)
        o_ref[1, :, t, :] = (z1i * tc - z1r * ts).astype(BF16)


def _dft_a(zr, zi, p, *, batch, n1, n2, fw):
    rows = zr.shape[0]
    cb = DFT_BATCH
    zr2 = zr.reshape(rows // n2, n2 * fw)
    zi2 = zi.reshape(rows // n2, n2 * fw)
    blk = pl.BlockSpec((n1, cb * fw), lambda b, j: (b, j))
    tw = pl.BlockSpec((cb, n1, LANES), lambda b, j: (j, 0, 0))
    return pl.pallas_call(
        functools.partial(_dft_a_kernel, fw=fw),
        grid=(batch, n2 // cb),
        in_specs=[blk, blk, pl.BlockSpec((2 * n1, 2 * n1), lambda b, j: (0, 0)), tw, tw],
        out_specs=pl.BlockSpec((None, 2, n1, cb, fw), lambda b, j: (b, 0, 0, j, 0)),
        out_shape=jax.ShapeDtypeStruct((batch, 2, n1, n2, fw), BF16),
        compiler_params=_cparams(("parallel", "parallel")),
        name="dft_rows_a",
    )(zr2, zi2, p["dft_a"], p["tw_c"], p["tw_s"])


def _dft_b_kernel(z_ref, fb_ref, o_ref):
    fb = fb_ref[...]
    n2, fw = z_ref.shape[2], z_ref.shape[3]
    for t in range(DFT_BATCH):
        o_ref[:, t, :] = _dot(fb, z_ref[:, t].reshape(2 * n2, fw)).astype(BF16)


def _dft_b(z2, p, *, rows, batch, n1, n2, fw):
    cb = DFT_BATCH
    y = pl.pallas_call(
        _dft_b_kernel,
        grid=(batch, n1 // cb),
        in_specs=[
            pl.BlockSpec((None, 2, cb, n2, fw), lambda b, j: (b, 0, j, 0, 0)),
            pl.BlockSpec((n2, 2 * n2), lambda b, j: (0, 0)),
        ],
        out_specs=pl.BlockSpec((n2, cb, fw), lambda b, j: (b, j, 0)),
        out_shape=jax.ShapeDtypeStruct((rows // n1, n1, fw), BF16),
        compiler_params=_cparams(("parallel", "parallel")),
        name="dft_rows_b",
    )(z2, p["dft_b"])
    return y.reshape(rows, fw)


def _attn_kernel(qt_ref, k_ref, kc_ref, vt_ref, vtc_ref, o_ref, s_ref, acc_ref, *, tk, unroll):
    qt = qt_ref[...]
    tq = qt.shape[1]
    per = tk // KEY_CHUNK
    n = k_ref.shape[0] // tk

    def scores(j, slot):
        s = _dot(k_ref[pl.ds(pl.multiple_of(j * tk, tk), tk), :], qt)
        s_ref[slot] = s
        return jnp.max(s, axis=0, keepdims=True)

    def absorb(load_s, vts, cmax, m, l):
        m_new = jnp.maximum(m, cmax)
        alpha = jnp.exp2(m - m_new)
        vt = vts[0] if len(vts) == 1 else jnp.concatenate(vts, axis=1)
        sums = []
        for c in range(0, tq, MXU_COLS):
            cs = slice(c, c + MXU_COLS)
            p = jnp.exp2(load_s(cs) - m_new[:, cs])
            sums.append(jnp.sum(p, axis=0, keepdims=True))
            acc_ref[:, cs] = alpha[:, cs] * acc_ref[:, cs] + _dot(vt, p.astype(BF16))
        return m_new, alpha * l + jnp.concatenate(sums, axis=1)

    def absorb_chunk(j, slot, cmax, m, l):
        return absorb(lambda cs: s_ref[slot, :, cs], [vt_ref[j * per + c] for c in range(per)], cmax, m, l)

    acc_ref[...] = jnp.zeros_like(acc_ref)
    sc = _dot(kc_ref[...], qt)
    cmax0 = scores(0, 0)
    m, l = absorb(lambda cs: sc[:, cs], [vtc_ref[0]], jnp.max(sc, axis=0, keepdims=True),
                  jnp.full((1, tq), -1e30, F32), jnp.zeros((1, tq), F32))

    def group(i, carry):
        m, l, cmax = carry
        for u in range(unroll):
            j = i * unroll + u
            cmax_next = scores(jnp.minimum(j + 1, n - 1), (u + 1) % 2)
            m, l = absorb_chunk(j, u % 2, cmax, m, l)
            cmax = cmax_next
        return m, l, cmax

    m, l, _ = lax.fori_loop(0, n // unroll, group, (m, l, cmax0))
    o_ref[...] = (acc_ref[...] / l).T.astype(BF16)


def _attention(qt, k, vt, *, batch, seq, ctx_len, heads):
    rows = k.shape[1]
    tq, tk = ATT_TQ, ATT_TK
    nq = seq // tq
    lat_rows = batch * seq
    return pl.pallas_call(
        functools.partial(_attn_kernel, tk=tk, unroll=math.gcd(ATT_UNROLL, seq // tk)),
        grid=(batch, heads, nq),
        in_specs=[
            pl.BlockSpec((None, QK_DIM, tq), lambda b, h, i: (h, 0, b * nq + i)),
            pl.BlockSpec((None, seq, QK_DIM), lambda b, h, i: (h, b, 0)),
            pl.BlockSpec((None, ctx_len, QK_DIM), lambda b, h, i: (h, lat_rows // ctx_len + b, 0)),
            pl.BlockSpec((None, seq // KEY_CHUNK, V_HEAD_DIM, KEY_CHUNK), lambda b, h, i: (h, b, 0, 0)),
            pl.BlockSpec((None, 1, V_HEAD_DIM, KEY_CHUNK), lambda b, h, i: (h, lat_rows // KEY_CHUNK + b, 0, 0)),
        ],
        out_specs=pl.BlockSpec((tq, V_HEAD_DIM), lambda b, h, i: (b * nq + i, h)),
        out_shape=jax.ShapeDtypeStruct((rows, heads * V_HEAD_DIM), BF16),
        scratch_shapes=[pltpu.VMEM((2, tk, tq), F32), pltpu.VMEM((V_HEAD_DIM, tq), F32)],
        compiler_params=_cparams(("parallel", "parallel", "arbitrary")),
        name="attention",
    )(qt, k, k, vt, vt)


def _ctx_kernel(qt_ref, k_ref, vt_ref, zr_ref, zi_ref, fc_ref, o_in, y_in, o_ref, y_ref, *, heads):
    del o_in, y_in
    for hd in range(heads):
        s = _dot(k_ref[hd], qt_ref[hd])
        p = jnp.exp2(s - jnp.max(s, axis=0, keepdims=True))
        l = jnp.sum(p, axis=0, keepdims=True)
        o = _dot(vt_ref[hd, 0], p.astype(BF16)) / l
        o_ref[:, hd * V_HEAD_DIM:(hd + 1) * V_HEAD_DIM] = o.T.astype(BF16)
    zst = jnp.concatenate([zr_ref[...], zi_ref[...]], axis=0)
    y_ref[...] = _dot(fc_ref[...], zst).astype(BF16)


def _ctx_mixer(qt, k, vt, zr, zi, o, y, p, *, batch, lat_rows, ctx_len, heads, fw):
    c0 = lat_rows // ctx_len
    return pl.pallas_call(
        functools.partial(_ctx_kernel, heads=heads),
        grid=(batch,),
        in_specs=[
            pl.BlockSpec((heads, QK_DIM, ctx_len), lambda b: (0, 0, c0 + b)),
            pl.BlockSpec((heads, ctx_len, QK_DIM), lambda b: (0, c0 + b, 0)),
            pl.BlockSpec((heads, 1, V_HEAD_DIM, KEY_CHUNK), lambda b: (0, c0 + b, 0, 0)),
            pl.BlockSpec((ctx_len, fw), lambda b: (c0 + b, 0)),
            pl.BlockSpec((ctx_len, fw), lambda b: (c0 + b, 0)),
            pl.BlockSpec((ctx_len, 2 * ctx_len), lambda b: (0, 0)),
            pl.BlockSpec(memory_space=pl.ANY),
            pl.BlockSpec(memory_space=pl.ANY),
        ],
        out_specs=[
            pl.BlockSpec((ctx_len, heads * V_HEAD_DIM), lambda b: (c0 + b, 0)),
            pl.BlockSpec((ctx_len, fw), lambda b: (c0 + b, 0)),
        ],
        out_shape=[jax.ShapeDtypeStruct(o.shape, o.dtype), jax.ShapeDtypeStruct(y.shape, y.dtype)],
        input_output_aliases={6: 0, 7: 1},
        compiler_params=_cparams(("parallel",)),
        name="ctx_mixer",
    )(qt, k, vt, zr, zi, p["dft_ctx"], o, y)


def _mixer_out_kernel(x_ref, mod_ref, a_ref, y_ref, wf_ref, wo_ref, o_ref, *, groups):
    ch = y_ref.shape[1] // groups
    parts = [a_ref[...]]
    for gi in range(groups):
        parts.append(_dot(y_ref[:, gi * ch:(gi + 1) * ch], wf_ref[gi]).astype(BF16))
    cat = jnp.concatenate(parts, axis=1)
    o_ref[...] = x_ref[...] + mod_ref[5:6, :] * _dot(cat, wo_ref[...])


def _mixer_out(xa, mods, o, y, p, layer, *, rows, seq, groups):
    d = xa.shape[1]
    tm = ROW_TILE
    aw, fw = o.shape[1], y.shape[1]
    ch = fw // groups
    row = lambda i: (i, 0)
    return pl.pallas_call(
        functools.partial(_mixer_out_kernel, groups=groups),
        grid=(rows // tm,),
        in_specs=[
            pl.BlockSpec((tm, d), row),
            pl.BlockSpec((None, N_MOD, d), lambda i: ((i * tm) // seq, 0, 0)),
            pl.BlockSpec((tm, aw), row),
            pl.BlockSpec((tm, fw), row),
            pl.BlockSpec((None, groups, ch, ch), lambda i: (layer, 0, 0, 0), pipeline_mode=pl.Buffered(1)),
            pl.BlockSpec((None, aw + fw, d), lambda i: (layer, 0, 0), pipeline_mode=pl.Buffered(1)),
        ],
        out_specs=pl.BlockSpec((tm, d), row),
        out_shape=jax.ShapeDtypeStruct((rows, d), F32),
        compiler_params=_cparams(("parallel",)),
        name="mixer_out",
    )(xa, mods, o, y, p["w_f"], p["w_o"])


def _dft_parts(n):
    idx = np.arange(n, dtype=np.int64)
    ang = 2.0 * np.pi * ((idx[:, None] * idx[None, :]) % n).astype(np.float64) / n
    s = 1.0 / math.sqrt(n)
    return np.cos(ang) * s, np.sin(ang) * s


def _tables(seq, ctx_len, ch, n1, n2, tm):
    t = {}
    c, s = _dft_parts(ch)
    t["dft_ch"] = jnp.asarray(np.concatenate([c, -s], axis=1), BF16)
    c, s = _dft_parts(n1)
    t["dft_a"] = jnp.asarray(np.block([[c, s], [-s, c]]), BF16)
    c, s = _dft_parts(n2)
    t["dft_b"] = jnp.asarray(np.concatenate([c, s], axis=1), BF16)
    c, s = _dft_parts(ctx_len)
    t["dft_ctx"] = jnp.asarray(np.concatenate([c, s], axis=1), BF16)
    i2 = np.arange(n2, dtype=np.int64)[:, None]
    k1 = np.arange(n1, dtype=np.int64)[None, :]
    ang = 2.0 * np.pi * ((i2 * k1) % seq).astype(np.float64) / seq
    t["tw_c"] = jnp.asarray(np.broadcast_to(np.cos(ang)[:, :, None], (n2, n1, LANES)), F32)
    t["tw_s"] = jnp.asarray(np.broadcast_to(np.sin(ang)[:, :, None], (n2, n1, LANES)), F32)

    rows = seq // GRID_W
    row = jnp.broadcast_to(jnp.arange(rows, dtype=F32)[:, None], (rows, GRID_W)).reshape(-1)
    col = jnp.broadcast_to(jnp.arange(GRID_W, dtype=F32)[None, :], (rows, GRID_W)).reshape(-1)
    inv_freq = ROPE_THETA ** (-jnp.arange(AXIS_FREQS, dtype=F32) / AXIS_FREQS)
    ang_r = row[:, None] * inv_freq
    ang_c = col[:, None] * inv_freq
    cr, sr, cc, sc = jnp.cos(ang_r), jnp.sin(ang_r), jnp.cos(ang_c), jnp.sin(ang_c)
    cos = jnp.concatenate([cr, cr, cc, cc], axis=1)
    sin = jnp.concatenate([-sr, sr, -sc, sc], axis=1)
    cos = jnp.concatenate([cos, jnp.ones((tm, QK_ROPE_DIM), F32)], axis=0)
    sin = jnp.concatenate([sin, jnp.zeros((tm, QK_ROPE_DIM), F32)], axis=0)
    pad = ((0, 0), (0, LANES - QK_ROPE_DIM))
    t["rope_ck"] = jnp.pad(cos, pad)
    t["rope_sk"] = jnp.pad(sin, pad)
    t["rope_cqt"] = cos.T
    t["rope_sqt"] = sin.T
    return t


def _swap_perm():
    a = AXIS_FREQS
    return np.concatenate([np.arange(a, 2 * a), np.arange(0, a), np.arange(3 * a, 4 * a), np.arange(2 * a, 3 * a)])


def kernel(x, c, ctx, c_ctx, w_ada, b_ada, norm_g, ffn1_w_gate, ffn1_w_up, ffn1_w_down, ffn2_w_gate, ffn2_w_up,
           ffn2_w_down, w_in, q_lat_g, w_uq, kv_lat_g, w_ukv, q_norm_g, k_norm_g, w_fourier, w_o):
    batch, seq, d = x.shape
    ctx_len = ctx.shape[1]
    depth = w_ada.shape[0]
    rq = q_lat_g.shape[1]
    rkv = kv_lat_g.shape[1]
    heads = w_uq.shape[2] // QK_DIM
    groups = w_fourier.shape[1]
    ch = w_fourier.shape[2]
    fw = groups * ch
    tm = ROW_TILE
    lat_rows = batch * seq
    ctx_rows = batch * ctx_len
    rows = lat_rows + ctx_rows
    n1 = 1 << ((seq.bit_length() - 1) // 2)
    n2 = seq // n1
    assert n1 * n2 == seq and n1 % DFT_BATCH == 0 and n2 % DFT_BATCH == 0
    assert ctx_len == KEY_CHUNK and seq % tm == 0 and ctx_rows % tm == 0 and ctx_rows <= seq
    assert rows % n1 == 0 and rows % n2 == 0 and seq % ATT_TQ == 0 and seq % (2 * ATT_TK) == 0
    assert batch + 1 <= COND_ROWS and w_in.shape[2] == rq + rkv + QK_ROPE_DIM + fw
    assert w_ukv.shape[2] == heads * (QK_NOPE_DIM + V_HEAD_DIM)

    perm = _swap_perm()
    k0 = rq + rkv
    zpad = jnp.zeros((depth, d, LANES - QK_ROPE_DIM), w_in.dtype)
    w_rope = w_in[:, :, k0:k0 + QK_ROPE_DIM]
    w_in_ext = jnp.concatenate(
        [w_in[:, :, :k0], w_in[:, :, k0 + QK_ROPE_DIM:], w_rope, zpad, w_rope[:, :, perm], zpad], axis=2)
    w_ukv4 = w_ukv.reshape(depth, rkv, heads, QK_NOPE_DIM + V_HEAD_DIM)
    gpad = ((0, 0), (0, 0), (0, LANES - QK_ROPE_DIM))
    g_rope = k_norm_g[:, None, QK_NOPE_DIM:]
    p = _tables(seq, ctx_len, ch, n1, n2, tm)
    p.update(
        w_in=w_in_ext.astype(BF16),
        q_lat_g=q_lat_g[:, None, :],
        w_uqt=jnp.swapaxes(w_uq, 1, 2).astype(BF16),
        gq=q_norm_g[:, :, None],
        kv_lat_g=kv_lat_g[:, None, :],
        w_uk=w_ukv4[..., :QK_NOPE_DIM].reshape(depth, rkv, heads * QK_NOPE_DIM).astype(BF16),
        w_uvt=jnp.swapaxes(w_ukv4[..., QK_NOPE_DIM:].reshape(depth, rkv, heads * V_HEAD_DIM), 1, 2).astype(BF16),
        gk_n=k_norm_g[:, None, :QK_NOPE_DIM],
        gk_r=jnp.pad(g_rope, gpad),
        gk_rs=jnp.pad(g_rope[:, :, perm], gpad),
        w_f=w_fourier.astype(BF16),
        w_o=w_o.astype(BF16),
    )
    ffn1 = tuple(w.astype(BF16) for w in (ffn1_w_gate, ffn1_w_up, ffn1_w_down))
    ffn2 = tuple(w.astype(BF16) for w in (ffn2_w_gate, ffn2_w_up, ffn2_w_down))

    cond = jnp.concatenate([c, c_ctx[None, :], jnp.zeros((COND_ROWS - batch - 1, d), c.dtype)], axis=0)
    mods_all = _adaln(cond, w_ada, b_ada).reshape(depth, COND_ROWS, N_MOD, d)

    xa = jnp.concatenate([x.reshape(lat_rows, d), ctx.reshape(ctx_rows, d)], axis=0)
    dims = (heads, rq, rkv, fw, groups)
    for layer in range(depth):
        last = layer == depth - 1
        mods = mods_all[layer]
        g = norm_g[layer]
        xa = _ffn(xa, mods, g[0:1], *ffn1, layer, mod0=0, rows=rows, seq=seq)
        qt, k, vt, zr, zi = _mixer_in(xa, mods, g[1:2], p, layer, seq=seq, lat_rows=lat_rows, dims=dims)
        z2 = _dft_a(zr, zi, p, batch=batch, n1=n1, n2=n2, fw=fw)
        y = _dft_b(z2, p, rows=rows, batch=batch, n1=n1, n2=n2, fw=fw)
        o = _attention(qt, k, vt, batch=batch, seq=seq, ctx_len=ctx_len, heads=heads)
        if not last:
            o, y = _ctx_mixer(qt, k, vt, zr, zi, o, y, p, batch=batch, lat_rows=lat_rows, ctx_len=ctx_len,
                              heads=heads, fw=fw)
        out_rows = lat_rows if last else rows
        xa = _mixer_out(xa, mods, o, y, p, layer, rows=out_rows, seq=seq, groups=groups)
        xa = _ffn(xa, mods, g[2:3], *ffn2, layer, mod0=6, rows=out_rows, seq=seq)
    return xa.reshape(batch, seq, d)
```

```python
import functools
import math

import jax
import jax.numpy as jnp
import numpy as np
from jax import lax
from jax.experimental import pallas as pl
from jax.experimental.pallas import tpu as pltpu

GRID_W = 64
V_HEAD_DIM = 128
QK_NOPE_DIM = 128
QK_ROPE_DIM = 64
QK_DIM = QK_NOPE_DIM + QK_ROPE_DIM
FOURIER_GROUPS = 4
N_MOD = 9
AXIS_FREQS = QK_ROPE_DIM // 4
ROPE_THETA = 10000.0
SM_SCALE = QK_DIM ** -0.5
RMS_EPS = 1e-6
LOG2E = 1.4426950408889634

LANES = 128
MXU_COLS = 256
BF16_ROWS = 16
COND_ROWS = 8
VMEM_LIMIT = 60 * 1024 * 1024

ROW_TILE = 512
KEY_CHUNK = 256
FFN_TILE = 1024
ADA_TILE = 1024
ATT_TQ = 1024
ATT_TK = 512
ATT_UNROLL = 16

BF16 = jnp.bfloat16
F32 = jnp.float32
U32 = jnp.uint32


def _pairs(x):
    return pltpu.bitcast(x, U32)


def _unpairs(u):
    return pltpu.bitcast(u, BF16)


def _dot(a, b):
    return jnp.dot(a, b, preferred_element_type=F32)


def _dot_nt(a, b):
    return lax.dot_general(a, b, (((1,), (1,)), ((), ())), preferred_element_type=F32)


def _cparams(sem):
    return pltpu.CompilerParams(dimension_semantics=sem, vmem_limit_bytes=VMEM_LIMIT)


def _norm_mod(x, g, shift, scale):
    y = x * lax.rsqrt(jnp.mean(x * x, axis=-1, keepdims=True) + RMS_EPS) * g
    return y * (1.0 + scale) + shift


def _adaln_kernel(c_ref, w_ref, b_ref, o_ref):
    c = c_ref[...]
    s = (c * jax.nn.sigmoid(c)).astype(BF16)
    o_ref[...] = _dot(s, w_ref[...].astype(BF16)) + b_ref[...]


def _adaln(cond, w_ada, b_ada):
    depth, d, n = w_ada.shape
    tn = math.gcd(ADA_TILE, d)
    return pl.pallas_call(
        _adaln_kernel,
        grid=(depth, n // tn),
        in_specs=[
            pl.BlockSpec((COND_ROWS, d), lambda l, j: (0, 0)),
            pl.BlockSpec((None, d, tn), lambda l, j: (l, 0, j)),
            pl.BlockSpec((None, 1, tn), lambda l, j: (l, 0, j)),
        ],
        out_specs=pl.BlockSpec((None, COND_ROWS, tn), lambda l, j: (l, 0, j)),
        out_shape=jax.ShapeDtypeStruct((depth, COND_ROWS, n), F32),
        compiler_params=_cparams(("arbitrary", "arbitrary")),
        name="adaln",
    )(cond, w_ada, b_ada.reshape(depth, 1, n))


def _ffn_kernel(x_ref, mod_ref, g_ref, wg_ref, wu_ref, wd_ref, wgt_ref, wut_ref, wdt_ref, o_ref, h_ref, acc_ref,
                *, mod0, n_wide):
    f = pl.program_id(1)

    def block(wg, wu, wd):
        h = h_ref[...]
        gate = _dot(h, wg[...])
        up = _dot(h, wu[...])
        a = (gate * jax.nn.sigmoid(gate) * up).astype(BF16)
        return _dot(a, wd[...])

    @pl.when(f == 0)
    def _():
        h = _norm_mod(x_ref[...], g_ref[...], mod_ref[mod0:mod0 + 1, :], mod_ref[mod0 + 1:mod0 + 2, :])
        h_ref[...] = h.astype(BF16)
        acc_ref[...] = block(wgt_ref, wut_ref, wdt_ref)

    @pl.when((f > 0) & (f < n_wide))
    def _():
        acc_ref[...] += block(wg_ref, wu_ref, wd_ref)

    @pl.when(f == n_wide)
    def _():
        total = acc_ref[...] + block(wg_ref, wu_ref, wd_ref)
        o_ref[...] = x_ref[...] + (0.5 * mod_ref[mod0 + 2:mod0 + 3, :]) * total


def _ffn(xa, mods, g, wg, wu, wd, layer, *, mod0, rows, seq):
    d = xa.shape[1]
    d_ff = wg.shape[2]
    tm, tf = ROW_TILE, FFN_TILE
    n_wide = (d_ff - 1) // tf
    tail = d_ff - n_wide * tf
    assert n_wide >= 1 and tail % LANES == 0 and (n_wide * tf) % tail == 0
    tail_blk = (n_wide * tf) // tail
    row = lambda i, f: (i, 0)
    wide = lambda i, f: (layer, 0, jnp.maximum(f - 1, 0))
    wide_t = lambda i, f: (layer, jnp.maximum(f - 1, 0), 0)
    once = pl.Buffered(1)
    return pl.pallas_call(
        functools.partial(_ffn_kernel, mod0=mod0, n_wide=n_wide),
        grid=(rows // tm, n_wide + 1),
        in_specs=[
            pl.BlockSpec((tm, d), row),
            pl.BlockSpec((None, N_MOD, d), lambda i, f: ((i * tm) // seq, 0, 0)),
            pl.BlockSpec((1, d), lambda i, f: (0, 0)),
            pl.BlockSpec((None, d, tf), wide),
            pl.BlockSpec((None, d, tf), wide),
            pl.BlockSpec((None, tf, d), wide_t),
            pl.BlockSpec((None, d, tail), lambda i, f: (layer, 0, tail_blk), pipeline_mode=once),
            pl.BlockSpec((None, d, tail), lambda i, f: (layer, 0, tail_blk), pipeline_mode=once),
            pl.BlockSpec((None, tail, d), lambda i, f: (layer, tail_blk, 0), pipeline_mode=once),
        ],
        out_specs=pl.BlockSpec((tm, d), row),
        out_shape=jax.ShapeDtypeStruct((rows, d), F32),
        scratch_shapes=[pltpu.VMEM((tm, d), BF16), pltpu.VMEM((tm, d), F32)],
        compiler_params=_cparams(("parallel", "arbitrary")),
        name="ffn",
    )(xa, mods, g, wg, wu, wd, wg, wu, wd)


def _mixer_in_kernel(x_ref, mod_ref, g_ref, win_ref, gql_ref, wuqt_ref, gq_ref, gkvl_ref, wuk_ref, wuvt_ref,
                     gkn_ref, gkr_ref, gkrs_ref, ck_ref, sk_ref, cq_ref, sq_ref, dft_ref,
                     qt_ref, k_ref, vt_ref, zr_ref, zi_ref, *, heads, rq, rkv, fw, groups):
    tm = x_ref.shape[0]
    h = _norm_mod(x_ref[...], g_ref[...], mod_ref[3:4, :], mod_ref[4:5, :]).astype(BF16)
    proj = _dot(h, win_ref[...])
    f0 = rq + rkv
    r0 = f0 + fw

    qlat = proj[:, :rq]
    qn = (qlat * lax.rsqrt(jnp.mean(qlat * qlat, axis=-1, keepdims=True) + RMS_EPS) * gql_ref[...]).astype(BF16)
    qt = _dot_nt(wuqt_ref[...], qn)
    gq = gq_ref[...]
    cq = cq_ref[...]
    sq = sq_ref[...]
    a = AXIS_FREQS
    for hd in range(heads):
        blk = qt[hd * QK_DIM:(hd + 1) * QK_DIM]
        r = lax.rsqrt(jnp.mean(blk * blk, axis=0, keepdims=True) + RMS_EPS) * (SM_SCALE * LOG2E)
        blk = blk * r * gq
        rp = blk[QK_NOPE_DIM:]
        sw = jnp.concatenate([rp[a:2 * a], rp[:a], rp[3 * a:], rp[2 * a:3 * a]], axis=0)
        qt_ref[hd, :QK_NOPE_DIM, :] = blk[:QK_NOPE_DIM].astype(BF16)
        qt_ref[hd, QK_NOPE_DIM:, :] = (rp * cq + sw * sq).astype(BF16)

    kvlat = proj[:, rq:f0]
    kvn = (kvlat * lax.rsqrt(jnp.mean(kvlat * kvlat, axis=-1, keepdims=True) + RMS_EPS) * gkvl_ref[...]).astype(BF16)
    knope = _dot(kvn, wuk_ref[...])
    vt = _dot_nt(wuvt_ref[...], kvn)
    for c in range(tm // KEY_CHUNK):
        piece = vt[:, c * KEY_CHUNK:(c + 1) * KEY_CHUNK].astype(BF16)
        vt_ref[:, c] = piece.reshape(heads, V_HEAD_DIM, KEY_CHUNK)
    kr = proj[:, r0:r0 + LANES]
    krs = proj[:, r0 + LANES:r0 + 2 * LANES]
    krr = kr * gkr_ref[...] * ck_ref[...] + krs * gkrs_ref[...] * sk_ref[...]
    ss_r = jnp.sum(kr * kr, axis=-1, keepdims=True)
    gkn = gkn_ref[...]
    for hd in range(heads):
        kn = knope[:, hd * QK_NOPE_DIM:(hd + 1) * QK_NOPE_DIM]
        ss = jnp.sum(kn * kn, axis=-1, keepdims=True) + ss_r
        r = lax.rsqrt(ss * (1.0 / QK_DIM) + RMS_EPS)
        k_ref[hd, :, :QK_NOPE_DIM] = (kn * r * gkn).astype(BF16)
        k_ref[hd, :, QK_NOPE_DIM:] = (krr * r)[:, :QK_ROPE_DIM].astype(BF16)

    ch = fw // groups
    fb = proj[:, f0:r0].astype(BF16)
    dft = dft_ref[...]
    for gi in range(groups):
        uw = _dot(fb[:, gi * ch:(gi + 1) * ch], dft)
        zr_ref[:, gi * ch:(gi + 1) * ch] = uw[:, :ch].astype(BF16)
        zi_ref[:, gi * ch:(gi + 1) * ch] = uw[:, ch:].astype(BF16)


def _mixer_in(xa, mods, g, p, layer, *, seq, lat_rows, dims):
    rows, d = xa.shape
    heads, rq, rkv, fw, groups = dims
    tm = ROW_TILE
    ncols = p["w_in"].shape[2]
    n_lat = lat_rows // tm
    per_seq = seq // tm
    const2 = lambda i: (0, 0)
    row = lambda i: (i, 0)
    pos = lambda i: (jnp.where(i < n_lat, i % per_seq, per_seq), 0)
    post = lambda i: (0, jnp.where(i < n_lat, i % per_seq, per_seq))
    lay = lambda *shape: pl.BlockSpec((None,) + shape, lambda i: (layer,) + (0,) * len(shape),
                                      pipeline_mode=pl.Buffered(1))
    ch = fw // groups
    out_shape = [
        jax.ShapeDtypeStruct((heads, QK_DIM, rows), BF16),
        jax.ShapeDtypeStruct((heads, rows, QK_DIM), BF16),
        jax.ShapeDtypeStruct((heads, rows // KEY_CHUNK, V_HEAD_DIM, KEY_CHUNK), BF16),
        jax.ShapeDtypeStruct((rows, fw), BF16),
        jax.ShapeDtypeStruct((rows, fw), BF16),
    ]
    out_specs = [
        pl.BlockSpec((heads, QK_DIM, tm), lambda i: (0, 0, i)),
        pl.BlockSpec((heads, tm, QK_DIM), lambda i: (0, i, 0)),
        pl.BlockSpec((heads, tm // KEY_CHUNK, V_HEAD_DIM, KEY_CHUNK), lambda i: (0, i, 0, 0)),
        pl.BlockSpec((tm, fw), row),
        pl.BlockSpec((tm, fw), row),
    ]
    in_specs = [
        pl.BlockSpec((tm, d), row),
        pl.BlockSpec((None, N_MOD, d), lambda i: ((i * tm) // seq, 0, 0)),
        pl.BlockSpec((1, d), const2),
        lay(d, ncols),
        lay(1, rq),
        lay(heads * QK_DIM, rq),
        lay(QK_DIM, 1),
        lay(1, rkv),
        lay(rkv, heads * QK_NOPE_DIM),
        lay(heads * V_HEAD_DIM, rkv),
        lay(1, LANES), lay(1, LANES), lay(1, LANES),
        pl.BlockSpec((tm, LANES), pos), pl.BlockSpec((tm, LANES), pos),
        pl.BlockSpec((QK_ROPE_DIM, tm), post), pl.BlockSpec((QK_ROPE_DIM, tm), post),
        pl.BlockSpec((ch, 2 * ch), const2),
    ]
    return pl.pallas_call(
        functools.partial(_mixer_in_kernel, heads=heads, rq=rq, rkv=rkv, fw=fw, groups=groups),
        grid=(rows // tm,),
        in_specs=in_specs,
        out_specs=out_specs,
        out_shape=out_shape,
        compiler_params=_cparams(("parallel",)),
        name="mixer_in",
    )(xa, mods, g, p["w_in"], p["q_lat_g"], p["w_uqt"], p["gq"], p["kv_lat_g"], p["w_uk"], p["w_uvt"],
      p["gk_n"], p["gk_r"], p["gk_rs"], p["rope_ck"], p["rope_sk"], p["rope_cqt"], p["rope_sqt"], p["dft_ch"])


DFT_BATCH = 16


def _bf16_row_of_pair(u, odd):
    bits = (u & jnp.uint32(0xFFFF0000)) if odd else (u << 16)
    return lax.bitcast_convert_type(bits, F32).astype(BF16)


def _pack_bf16_row_pair(even, odd):
    lo = lax.bitcast_convert_type(even.astype(BF16).astype(F32), U32) >> 16
    hi = lax.bitcast_convert_type(odd.astype(BF16).astype(F32), U32)
    return lo | hi


def _dft_a_kernel(zr_ref, zi_ref, fa_ref, tc_ref, ts_ref, o_ref, *, fw):
    n1 = fa_ref.shape[0] // 2
    fa = fa_ref[...]
    reps = fw // LANES
    for t2 in range(DFT_BATCH // 2):
        res = []
        for odd in (0, 1):
            t = 2 * t2 + odd
            cols = slice(t * fw, (t + 1) * fw)
            zst = jnp.concatenate([zr_ref[:, cols], zi_ref[:, cols]], axis=0)
            z1 = _dot(fa, zst)
            z1r, z1i = z1[:n1], z1[n1:]
            tc = jnp.concatenate([tc_ref[t]] * reps, axis=1)
            ts = jnp.concatenate([ts_ref[t]] * reps, axis=1)
            res.append((z1r * tc + z1i * ts, z1i * tc - z1r * ts))
        o_ref[0, :, t2, :] = _pack_bf16_row_pair(res[0][0], res[1][0])
        o_ref[1, :, t2, :] = _pack_bf16_row_pair(res[0][1], res[1][1])


def _dft_a(zr, zi, p, *, batch, n1, n2, fw):
    rows = zr.shape[0]
    cb = DFT_BATCH
    zr2 = zr.reshape(rows // n2, n2 * fw)
    zi2 = zi.reshape(rows // n2, n2 * fw)
    blk = pl.BlockSpec((n1, cb * fw), lambda b, j: (b, j))
    tw = pl.BlockSpec((cb, n1, LANES), lambda b, j: (j, 0, 0))
    return pl.pallas_call(
        functools.partial(_dft_a_kernel, fw=fw),
        grid=(batch, n2 // cb),
        in_specs=[blk, blk, pl.BlockSpec((2 * n1, 2 * n1), lambda b, j: (0, 0)), tw, tw],
        out_specs=pl.BlockSpec((None, 2, n1, cb // 2, fw), lambda b, j: (b, 0, 0, j, 0)),
        out_shape=jax.ShapeDtypeStruct((batch, 2, n1, n2 // 2, fw), U32),
        compiler_params=_cparams(("parallel", "parallel")),
        name="dft_rows_a",
    )(zr2, zi2, p["dft_a"], p["tw_c"], p["tw_s"])


def _dft_b_kernel(z_ref, fb_ref, o_ref):
    fb = fb_ref[...]
    fw = z_ref.shape[3]
    for t2 in range(DFT_BATCH // 2):
        even, odd = (_dot(fb, _unpairs(z_ref[:, 2 * t2 + h]).reshape(-1, fw)) for h in (0, 1))
        o_ref[:, t2, :] = _pack_bf16_row_pair(even, odd)


def _dft_b(z2, p, *, rows, batch, n1, n2, fw):
    cb = DFT_BATCH
    y = pl.pallas_call(
        _dft_b_kernel,
        grid=(batch, n1 // cb),
        in_specs=[
            pl.BlockSpec((None, 2, cb, n2 // 2, fw), lambda b, j: (b, 0, j, 0, 0)),
            pl.BlockSpec((n2, 2 * n2), lambda b, j: (0, 0)),
        ],
        out_specs=pl.BlockSpec((n2, cb // 2, fw), lambda b, j: (b, j, 0)),
        out_shape=jax.ShapeDtypeStruct((rows // n1, n1 // 2, fw), U32),
        compiler_params=_cparams(("parallel", "parallel")),
        name="dft_rows_b",
    )(z2, p["dft_b"])
    return y.reshape(rows // 2, fw)


def _attn_kernel(qt_ref, k_ref, kc_ref, vt_ref, vtc_ref, o_ref, s_ref, acc_ref, *, tk, unroll):
    tq = qt_ref.shape[1]
    per = tk // KEY_CHUNK
    n = k_ref.shape[0] // tk

    def scores(j, slot):
        s = _dot(k_ref[pl.ds(pl.multiple_of(j * tk, tk), tk), :], qt_ref[...])
        s_ref[slot] = s
        return jnp.max(s, axis=0, keepdims=True)

    def absorb(load_s, vts, cmax, m):
        m_new = jnp.maximum(m, cmax)
        alpha = jnp.exp2(m - m_new)
        vt = vts[0] if len(vts) == 1 else jnp.concatenate(vts, axis=1)
        vt = jnp.concatenate([vt, jnp.ones((BF16_ROWS, vt.shape[1]), BF16)], axis=0)
        for c in range(0, tq, MXU_COLS):
            cs = slice(c, c + MXU_COLS)
            p = jnp.exp2(load_s(cs) - m_new[:, cs])
            acc_ref[:, cs] = alpha[:, cs] * acc_ref[:, cs] + _dot(vt, p.astype(BF16))
        return m_new

    def absorb_chunk(j, slot, cmax, m):
        return absorb(lambda cs: s_ref[slot, :, cs], [vt_ref[j * per + c] for c in range(per)], cmax, m)

    acc_ref[...] = jnp.zeros_like(acc_ref)
    sc = _dot(kc_ref[...], qt_ref[...])
    cmax0 = scores(0, 0)
    m = absorb(lambda cs: sc[:, cs], [vtc_ref[0]], jnp.max(sc, axis=0, keepdims=True),
               jnp.full((1, tq), -1e30, F32))

    def group(i, carry):
        m, cmax = carry
        for u in range(unroll):
            j = i * unroll + u
            cmax_next = scores(jnp.minimum(j + 1, n - 1), (u + 1) % 2)
            m = absorb_chunk(j, u % 2, cmax, m)
            cmax = cmax_next
        return m, cmax

    lax.fori_loop(0, n // unroll, group, (m, cmax0))
    o_ref[...] = (acc_ref[:V_HEAD_DIM] / acc_ref[V_HEAD_DIM:V_HEAD_DIM + 1]).T.astype(BF16)


def _attention(qt, k, vt, *, batch, seq, ctx_len, heads):
    rows = k.shape[1]
    tq, tk = ATT_TQ, ATT_TK
    nq = seq // tq
    lat_rows = batch * seq
    return pl.pallas_call(
        functools.partial(_attn_kernel, tk=tk, unroll=math.gcd(ATT_UNROLL, seq // tk)),
        grid=(batch, heads, nq),
        in_specs=[
            pl.BlockSpec((None, QK_DIM, tq), lambda b, h, i: (h, 0, b * nq + i)),
            pl.BlockSpec((None, seq, QK_DIM), lambda b, h, i: (h, b, 0)),
            pl.BlockSpec((None, ctx_len, QK_DIM), lambda b, h, i: (h, lat_rows // ctx_len + b, 0)),
            pl.BlockSpec((None, seq // KEY_CHUNK, V_HEAD_DIM, KEY_CHUNK), lambda b, h, i: (h, b, 0, 0)),
            pl.BlockSpec((None, 1, V_HEAD_DIM, KEY_CHUNK), lambda b, h, i: (h, lat_rows // KEY_CHUNK + b, 0, 0)),
        ],
        out_specs=pl.BlockSpec((tq, V_HEAD_DIM), lambda b, h, i: (b * nq + i, h)),
        out_shape=jax.ShapeDtypeStruct((rows, heads * V_HEAD_DIM), BF16),
        scratch_shapes=[pltpu.VMEM((2, tk, tq), F32), pltpu.VMEM((V_HEAD_DIM + BF16_ROWS, tq), F32)],
        compiler_params=_cparams(("parallel", "parallel", "arbitrary")),
        name="attention",
    )(qt, k, k, vt, vt)


def _ctx_kernel(qt_ref, k_ref, vt_ref, zr_ref, zi_ref, fc_ref, o_in, y_in, o_ref, y_ref, *, heads):
    del o_in, y_in
    for hd in range(heads):
        s = _dot(k_ref[hd], qt_ref[hd])
        p = jnp.exp2(s - jnp.max(s, axis=0, keepdims=True))
        l = jnp.sum(p, axis=0, keepdims=True)
        o = _dot(vt_ref[hd, 0], p.astype(BF16)) / l
        o_ref[:, hd * V_HEAD_DIM:(hd + 1) * V_HEAD_DIM] = o.T.astype(BF16)
    zst = jnp.concatenate([zr_ref[...], zi_ref[...]], axis=0)
    y_ref[...] = _pairs(_dot(fc_ref[...], zst).astype(BF16))


def _ctx_mixer(qt, k, vt, zr, zi, o, y, p, *, batch, lat_rows, ctx_len, heads, fw):
    c0 = lat_rows // ctx_len
    return pl.pallas_call(
        functools.partial(_ctx_kernel, heads=heads),
        grid=(batch,),
        in_specs=[
            pl.BlockSpec((heads, QK_DIM, ctx_len), lambda b: (0, 0, c0 + b)),
            pl.BlockSpec((heads, ctx_len, QK_DIM), lambda b: (0, c0 + b, 0)),
            pl.BlockSpec((heads, 1, V_HEAD_DIM, KEY_CHUNK), lambda b: (0, c0 + b, 0, 0)),
            pl.BlockSpec((ctx_len, fw), lambda b: (c0 + b, 0)),
            pl.BlockSpec((ctx_len, fw), lambda b: (c0 + b, 0)),
            pl.BlockSpec((ctx_len, 2 * ctx_len), lambda b: (0, 0)),
            pl.BlockSpec(memory_space=pl.ANY),
            pl.BlockSpec(memory_space=pl.ANY),
        ],
        out_specs=[
            pl.BlockSpec((ctx_len, heads * V_HEAD_DIM), lambda b: (c0 + b, 0)),
            pl.BlockSpec((ctx_len // 2, fw), lambda b: (c0 + b, 0)),
        ],
        out_shape=[jax.ShapeDtypeStruct(o.shape, o.dtype), jax.ShapeDtypeStruct(y.shape, y.dtype)],
        input_output_aliases={6: 0, 7: 1},
        compiler_params=_cparams(("parallel",)),
        name="ctx_mixer",
    )(qt, k, vt, zr, zi, p["dft_ctx"], o, y)


def _mixer_out_kernel(x_ref, mod_ref, a_ref, y_ref, wf_ref, wo_ref, o_ref, *, groups):
    ch = y_ref.shape[1] // groups
    parts = [a_ref[...]]
    for gi in range(groups):
        y = _unpairs(y_ref[:, gi * ch:(gi + 1) * ch])
        parts.append(_dot(y, wf_ref[gi]).astype(BF16))
    cat = jnp.concatenate(parts, axis=1)
    o_ref[...] = x_ref[...] + mod_ref[5:6, :] * _dot(cat, wo_ref[...])


def _mixer_out(xa, mods, o, y, p, layer, *, rows, seq, groups):
    d = xa.shape[1]
    tm = ROW_TILE
    aw, fw = o.shape[1], y.shape[1]
    ch = fw // groups
    row = lambda i: (i, 0)
    return pl.pallas_call(
        functools.partial(_mixer_out_kernel, groups=groups),
        grid=(rows // tm,),
        in_specs=[
            pl.BlockSpec((tm, d), row),
            pl.BlockSpec((None, N_MOD, d), lambda i: ((i * tm) // seq, 0, 0)),
            pl.BlockSpec((tm, aw), row),
            pl.BlockSpec((tm // 2, fw), row),
            pl.BlockSpec((None, groups, ch, ch), lambda i: (layer, 0, 0, 0), pipeline_mode=pl.Buffered(1)),
            pl.BlockSpec((None, aw + fw, d), lambda i: (layer, 0, 0), pipeline_mode=pl.Buffered(1)),
        ],
        out_specs=pl.BlockSpec((tm, d), row),
        out_shape=jax.ShapeDtypeStruct((rows, d), F32),
        compiler_params=_cparams(("parallel",)),
        name="mixer_out",
    )(xa, mods, o, y, p["w_f"], p["w_o"])


def _dft_parts(n):
    idx = np.arange(n, dtype=np.int64)
    ang = 2.0 * np.pi * ((idx[:, None] * idx[None, :]) % n).astype(np.float64) / n
    s = 1.0 / math.sqrt(n)
    return np.cos(ang) * s, np.sin(ang) * s


def _tables(seq, ctx_len, ch, n1, n2, tm):
    t = {}
    c, s = _dft_parts(ch)
    t["dft_ch"] = jnp.asarray(np.concatenate([c, -s], axis=1), BF16)
    c, s = _dft_parts(n1)
    t["dft_a"] = jnp.asarray(np.block([[c, s], [-s, c]]), BF16)
    c, s = _dft_parts(n2)
    t["dft_b"] = jnp.asarray(np.concatenate([c, s], axis=1), BF16)
    c, s = _dft_parts(ctx_len)
    t["dft_ctx"] = jnp.asarray(np.concatenate([c, s], axis=1), BF16)
    i2 = np.arange(n2, dtype=np.int64)[:, None]
    k1 = np.arange(n1, dtype=np.int64)[None, :]
    ang = 2.0 * np.pi * ((i2 * k1) % seq).astype(np.float64) / seq
    t["tw_c"] = jnp.asarray(np.broadcast_to(np.cos(ang)[:, :, None], (n2, n1, LANES)), F32)
    t["tw_s"] = jnp.asarray(np.broadcast_to(np.sin(ang)[:, :, None], (n2, n1, LANES)), F32)

    rows = seq // GRID_W
    row = jnp.broadcast_to(jnp.arange(rows, dtype=F32)[:, None], (rows, GRID_W)).reshape(-1)
    col = jnp.broadcast_to(jnp.arange(GRID_W, dtype=F32)[None, :], (rows, GRID_W)).reshape(-1)
    inv_freq = ROPE_THETA ** (-jnp.arange(AXIS_FREQS, dtype=F32) / AXIS_FREQS)
    ang_r = row[:, None] * inv_freq
    ang_c = col[:, None] * inv_freq
    cr, sr, cc, sc = jnp.cos(ang_r), jnp.sin(ang_r), jnp.cos(ang_c), jnp.sin(ang_c)
    cos = jnp.concatenate([cr, cr, cc, cc], axis=1)
    sin = jnp.concatenate([-sr, sr, -sc, sc], axis=1)
    cos = jnp.concatenate([cos, jnp.ones((tm, QK_ROPE_DIM), F32)], axis=0)
    sin = jnp.concatenate([sin, jnp.zeros((tm, QK_ROPE_DIM), F32)], axis=0)
    pad = ((0, 0), (0, LANES - QK_ROPE_DIM))
    t["rope_ck"] = jnp.pad(cos, pad)
    t["rope_sk"] = jnp.pad(sin, pad)
    t["rope_cqt"] = cos.T
    t["rope_sqt"] = sin.T
    return t


def _swap_perm():
    a = AXIS_FREQS
    return np.concatenate([np.arange(a, 2 * a), np.arange(0, a), np.arange(3 * a, 4 * a), np.arange(2 * a, 3 * a)])


def kernel(x, c, ctx, c_ctx, w_ada, b_ada, norm_g, ffn1_w_gate, ffn1_w_up, ffn1_w_down, ffn2_w_gate, ffn2_w_up,
           ffn2_w_down, w_in, q_lat_g, w_uq, kv_lat_g, w_ukv, q_norm_g, k_norm_g, w_fourier, w_o):
    batch, seq, d = x.shape
    ctx_len = ctx.shape[1]
    depth = w_ada.shape[0]
    rq = q_lat_g.shape[1]
    rkv = kv_lat_g.shape[1]
    heads = w_uq.shape[2] // QK_DIM
    groups = w_fourier.shape[1]
    ch = w_fourier.shape[2]
    fw = groups * ch
    tm = ROW_TILE
    lat_rows = batch * seq
    ctx_rows = batch * ctx_len
    rows = lat_rows + ctx_rows
    n1 = 1 << ((seq.bit_length() - 1) // 2)
    n2 = seq // n1
    assert n1 * n2 == seq and n1 % DFT_BATCH == 0 and n2 % DFT_BATCH == 0
    assert ctx_len == KEY_CHUNK and seq % tm == 0 and ctx_rows % tm == 0 and ctx_rows <= seq
    assert rows % n1 == 0 and rows % n2 == 0 and seq % ATT_TQ == 0 and seq % (2 * ATT_TK) == 0
    assert batch + 1 <= COND_ROWS and w_in.shape[2] == rq + rkv + QK_ROPE_DIM + fw
    assert w_ukv.shape[2] == heads * (QK_NOPE_DIM + V_HEAD_DIM)

    perm = _swap_perm()
    k0 = rq + rkv
    zpad = jnp.zeros((depth, d, LANES - QK_ROPE_DIM), w_in.dtype)
    w_rope = w_in[:, :, k0:k0 + QK_ROPE_DIM]
    w_in_ext = jnp.concatenate(
        [w_in[:, :, :k0], w_in[:, :, k0 + QK_ROPE_DIM:], w_rope, zpad, w_rope[:, :, perm], zpad], axis=2)
    w_ukv4 = w_ukv.reshape(depth, rkv, heads, QK_NOPE_DIM + V_HEAD_DIM)
    gpad = ((0, 0), (0, 0), (0, LANES - QK_ROPE_DIM))
    g_rope = k_norm_g[:, None, QK_NOPE_DIM:]
    p = _tables(seq, ctx_len, ch, n1, n2, tm)
    p.update(
        w_in=w_in_ext.astype(BF16),
        q_lat_g=q_lat_g[:, None, :],
        w_uqt=jnp.swapaxes(w_uq, 1, 2).astype(BF16),
        gq=q_norm_g[:, :, None],
        kv_lat_g=kv_lat_g[:, None, :],
        w_uk=w_ukv4[..., :QK_NOPE_DIM].reshape(depth, rkv, heads * QK_NOPE_DIM).astype(BF16),
        w_uvt=jnp.swapaxes(w_ukv4[..., QK_NOPE_DIM:].reshape(depth, rkv, heads * V_HEAD_DIM), 1, 2).astype(BF16),
        gk_n=k_norm_g[:, None, :QK_NOPE_DIM],
        gk_r=jnp.pad(g_rope, gpad),
        gk_rs=jnp.pad(g_rope[:, :, perm], gpad),
        w_f=w_fourier.astype(BF16),
        w_o=w_o.astype(BF16),
    )
    ffn1 = tuple(w.astype(BF16) for w in (ffn1_w_gate, ffn1_w_up, ffn1_w_down))
    ffn2 = tuple(w.astype(BF16) for w in (ffn2_w_gate, ffn2_w_up, ffn2_w_down))

    cond = jnp.concatenate([c, c_ctx[None, :], jnp.zeros((COND_ROWS - batch - 1, d), c.dtype)], axis=0)
    mods_all = _adaln(cond, w_ada, b_ada).reshape(depth, COND_ROWS, N_MOD, d)

    xa = jnp.concatenate([x.reshape(lat_rows, d), ctx.reshape(ctx_rows, d)], axis=0)
    dims = (heads, rq, rkv, fw, groups)
    for layer in range(depth):
        last = layer == depth - 1
        mods = mods_all[layer]
        g = norm_g[layer]
        xa = _ffn(xa, mods, g[0:1], *ffn1, layer, mod0=0, rows=rows, seq=seq)
        qt, k, vt, zr, zi = _mixer_in(xa, mods, g[1:2], p, layer, seq=seq, lat_rows=lat_rows, dims=dims)
        z2 = _dft_a(zr, zi, p, batch=batch, n1=n1, n2=n2, fw=fw)
        y = _dft_b(z2, p, rows=rows, batch=batch, n1=n1, n2=n2, fw=fw)
        o = _attention(qt, k, vt, batch=batch, seq=seq, ctx_len=ctx_len, heads=heads)
        if not last:
            o, y = _ctx_mixer(qt, k, vt, zr, zi, o, y, p, batch=batch, lat_rows=lat_rows, ctx_len=ctx_len,
                              heads=heads, fw=fw)
        out_rows = lat_rows if last else rows
        xa = _mixer_out(xa, mods, o, y, p, layer, rows=out_rows, seq=seq, groups=groups)
        xa = _ffn(xa, mods, g[2:3], *ffn2, layer, mod0=6, rows=out_rows, seq=seq)
    return xa.reshape(batch, seq, d)
```

```python
import functools
import math

import jax
import jax.numpy as jnp
import numpy as np
from jax import lax
from jax.experimental import pallas as pl
from jax.experimental.pallas import tpu as pltpu

GRID_W = 64
V_HEAD_DIM = 128
QK_NOPE_DIM = 128
QK_ROPE_DIM = 64
QK_DIM = QK_NOPE_DIM + QK_ROPE_DIM
N_MOD = 9
AXIS_FREQS = QK_ROPE_DIM // 4
ROPE_THETA = 10000.0
SM_SCALE = QK_DIM ** -0.5
RMS_EPS = 1e-6
LOG2E = 1.4426950408889634

LANES = 128
MXU_COLS = 256
BF16_ROWS = 16
COND_ROWS = 8
VMEM_LIMIT = 60 * 1024 * 1024

ROW_TILE = 512
KEY_CHUNK = 256
FFN_TILE = 1024
ADA_TILE = 1024
ATT_TQ = 1024
ATT_TK = 512
ATT_UNROLL = 16

BF16 = jnp.bfloat16
F32 = jnp.float32
U32 = jnp.uint32


def _pairs(x):
    return pltpu.bitcast(x, U32)


def _unpairs(u):
    return pltpu.bitcast(u, BF16)


def _dot(a, b):
    return jnp.dot(a, b, preferred_element_type=F32)


def _dot_nt(a, b):
    return lax.dot_general(a, b, (((1,), (1,)), ((), ())), preferred_element_type=F32)


def _cparams(sem):
    return pltpu.CompilerParams(dimension_semantics=sem, vmem_limit_bytes=VMEM_LIMIT)


def _norm_mod(x, g, shift, scale):
    y = x * lax.rsqrt(jnp.mean(x * x, axis=-1, keepdims=True) + RMS_EPS) * g
    return y * (1.0 + scale) + shift


def _adaln_kernel(c_ref, w_ref, b_ref, o_ref):
    c = c_ref[...]
    s = (c * jax.nn.sigmoid(c)).astype(BF16)
    o_ref[...] = _dot(s, w_ref[...].astype(BF16)) + b_ref[...]


def _adaln(cond, w_ada, b_ada):
    depth, d, n = w_ada.shape
    tn = math.gcd(ADA_TILE, d)
    return pl.pallas_call(
        _adaln_kernel,
        grid=(depth, n // tn),
        in_specs=[
            pl.BlockSpec((COND_ROWS, d), lambda l, j: (0, 0)),
            pl.BlockSpec((None, d, tn), lambda l, j: (l, 0, j)),
            pl.BlockSpec((None, 1, tn), lambda l, j: (l, 0, j)),
        ],
        out_specs=pl.BlockSpec((None, COND_ROWS, tn), lambda l, j: (l, 0, j)),
        out_shape=jax.ShapeDtypeStruct((depth, COND_ROWS, n), F32),
        compiler_params=_cparams(("arbitrary", "arbitrary")),
        name="adaln",
    )(cond, w_ada, b_ada.reshape(depth, 1, n))


def _ffn_kernel(x_ref, mod_ref, g_ref, wg_ref, wu_ref, wd_ref, wgt_ref, wut_ref, wdt_ref, o_ref, h_ref, acc_ref,
                *, mod0, n_wide):
    f = pl.program_id(1)

    def block(wg, wu, wd):
        h = h_ref[...]
        gate = _dot(h, wg[...])
        up = _dot(h, wu[...])
        a = (gate * jax.nn.sigmoid(gate) * up).astype(BF16)
        return _dot(a, wd[...])

    @pl.when(f == 0)
    def _():
        h = _norm_mod(x_ref[...], g_ref[...], mod_ref[mod0:mod0 + 1, :], mod_ref[mod0 + 1:mod0 + 2, :])
        h_ref[...] = h.astype(BF16)
        acc_ref[...] = block(wgt_ref, wut_ref, wdt_ref)

    @pl.when((f > 0) & (f < n_wide))
    def _():
        acc_ref[...] += block(wg_ref, wu_ref, wd_ref)

    @pl.when(f == n_wide)
    def _():
        total = acc_ref[...] + block(wg_ref, wu_ref, wd_ref)
        o_ref[...] = x_ref[...] + (0.5 * mod_ref[mod0 + 2:mod0 + 3, :]) * total


def _ffn(xa, mods, g, wg, wu, wd, layer, *, mod0, rows, seq):
    d = xa.shape[1]
    d_ff = wg.shape[2]
    tm, tf = ROW_TILE, FFN_TILE
    n_wide = (d_ff - 1) // tf
    tail = d_ff - n_wide * tf
    assert n_wide >= 1 and tail % LANES == 0 and (n_wide * tf) % tail == 0
    tail_blk = (n_wide * tf) // tail
    row = lambda i, f: (i, 0)
    wide = lambda i, f: (layer, 0, jnp.maximum(f - 1, 0))
    wide_t = lambda i, f: (layer, jnp.maximum(f - 1, 0), 0)
    once = pl.Buffered(1)
    return pl.pallas_call(
        functools.partial(_ffn_kernel, mod0=mod0, n_wide=n_wide),
        grid=(rows // tm, n_wide + 1),
        in_specs=[
            pl.BlockSpec((tm, d), row),
            pl.BlockSpec((None, N_MOD, d), lambda i, f: ((i * tm) // seq, 0, 0)),
            pl.BlockSpec((1, d), lambda i, f: (0, 0)),
            pl.BlockSpec((None, d, tf), wide),
            pl.BlockSpec((None, d, tf), wide),
            pl.BlockSpec((None, tf, d), wide_t),
            pl.BlockSpec((None, d, tail), lambda i, f: (layer, 0, tail_blk), pipeline_mode=once),
            pl.BlockSpec((None, d, tail), lambda i, f: (layer, 0, tail_blk), pipeline_mode=once),
            pl.BlockSpec((None, tail, d), lambda i, f: (layer, tail_blk, 0), pipeline_mode=once),
        ],
        out_specs=pl.BlockSpec((tm, d), row),
        out_shape=jax.ShapeDtypeStruct((rows, d), F32),
        scratch_shapes=[pltpu.VMEM((tm, d), BF16), pltpu.VMEM((tm, d), F32)],
        compiler_params=_cparams(("parallel", "arbitrary")),
        name="ffn",
    )(xa, mods, g, wg, wu, wd, wg, wu, wd)


def _mixer_in_kernel(x_ref, mod_ref, g_ref, win_ref, gql_ref, wuqt_ref, gq_ref, gkvl_ref, wuk_ref, wuvt_ref,
                     gkn_ref, gkr_ref, gkrs_ref, ck_ref, sk_ref, cq_ref, sq_ref, dft_ref,
                     qt_ref, k_ref, vt_ref, zr_ref, zi_ref, *, heads, rq, rkv, fw, groups):
    tm = x_ref.shape[0]
    h = _norm_mod(x_ref[...], g_ref[...], mod_ref[3:4, :], mod_ref[4:5, :]).astype(BF16)
    proj = _dot(h, win_ref[...])
    f0 = rq + rkv
    r0 = f0 + fw

    qlat = proj[:, :rq]
    qn = (qlat * lax.rsqrt(jnp.mean(qlat * qlat, axis=-1, keepdims=True) + RMS_EPS) * gql_ref[...]).astype(BF16)
    qt = _dot_nt(wuqt_ref[...], qn)
    gq = gq_ref[...]
    cq = cq_ref[...]
    sq = sq_ref[...]
    a = AXIS_FREQS
    for hd in range(heads):
        blk = qt[hd * QK_DIM:(hd + 1) * QK_DIM]
        r = lax.rsqrt(jnp.mean(blk * blk, axis=0, keepdims=True) + RMS_EPS) * (SM_SCALE * LOG2E)
        blk = blk * r * gq
        rp = blk[QK_NOPE_DIM:]
        sw = jnp.concatenate([rp[a:2 * a], rp[:a], rp[3 * a:], rp[2 * a:3 * a]], axis=0)
        qt_ref[hd, :QK_NOPE_DIM, :] = blk[:QK_NOPE_DIM].astype(BF16)
        qt_ref[hd, QK_NOPE_DIM:, :] = (rp * cq + sw * sq).astype(BF16)

    kvlat = proj[:, rq:f0]
    kvn = (kvlat * lax.rsqrt(jnp.mean(kvlat * kvlat, axis=-1, keepdims=True) + RMS_EPS) * gkvl_ref[...]).astype(BF16)
    knope = _dot(kvn, wuk_ref[...])
    vt = _dot_nt(wuvt_ref[...], kvn)
    for c in range(tm // KEY_CHUNK):
        piece = vt[:, c * KEY_CHUNK:(c + 1) * KEY_CHUNK].astype(BF16)
        vt_ref[:, c] = piece.reshape(heads, V_HEAD_DIM, KEY_CHUNK)
    kr = proj[:, r0:r0 + LANES]
    krs = proj[:, r0 + LANES:r0 + 2 * LANES]
    krr = kr * gkr_ref[...] * ck_ref[...] + krs * gkrs_ref[...] * sk_ref[...]
    ss_r = jnp.sum(kr * kr, axis=-1, keepdims=True)
    gkn = gkn_ref[...]
    for hd in range(heads):
        kn = knope[:, hd * QK_NOPE_DIM:(hd + 1) * QK_NOPE_DIM]
        ss = jnp.sum(kn * kn, axis=-1, keepdims=True) + ss_r
        r = lax.rsqrt(ss * (1.0 / QK_DIM) + RMS_EPS)
        k_ref[hd, :, :QK_NOPE_DIM] = (kn * r * gkn).astype(BF16)
        k_ref[hd, :, QK_NOPE_DIM:] = (krr * r)[:, :QK_ROPE_DIM].astype(BF16)

    ch = fw // groups
    fb = proj[:, f0:r0].astype(BF16)
    dft = dft_ref[...]
    for gi in range(groups):
        uw = _dot(fb[:, gi * ch:(gi + 1) * ch], dft)
        zr_ref[:, gi * ch:(gi + 1) * ch] = uw[:, :ch].astype(BF16)
        zi_ref[:, gi * ch:(gi + 1) * ch] = uw[:, ch:].astype(BF16)


def _mixer_in(xa, mods, g, p, layer, *, seq, lat_rows, dims):
    rows, d = xa.shape
    heads, rq, rkv, fw, groups = dims
    tm = ROW_TILE
    ncols = p["w_in"].shape[2]
    n_lat = lat_rows // tm
    per_seq = seq // tm
    const2 = lambda i: (0, 0)
    row = lambda i: (i, 0)
    pos = lambda i: (jnp.where(i < n_lat, i % per_seq, per_seq), 0)
    post = lambda i: (0, jnp.where(i < n_lat, i % per_seq, per_seq))
    lay = lambda *shape: pl.BlockSpec((None,) + shape, lambda i: (layer,) + (0,) * len(shape),
                                      pipeline_mode=pl.Buffered(1))
    ch = fw // groups
    out_shape = [
        jax.ShapeDtypeStruct((heads, QK_DIM, rows), BF16),
        jax.ShapeDtypeStruct((heads, rows, QK_DIM), BF16),
        jax.ShapeDtypeStruct((heads, rows // KEY_CHUNK, V_HEAD_DIM, KEY_CHUNK), BF16),
        jax.ShapeDtypeStruct((rows, fw), BF16),
        jax.ShapeDtypeStruct((rows, fw), BF16),
    ]
    out_specs = [
        pl.BlockSpec((heads, QK_DIM, tm), lambda i: (0, 0, i)),
        pl.BlockSpec((heads, tm, QK_DIM), lambda i: (0, i, 0)),
        pl.BlockSpec((heads, tm // KEY_CHUNK, V_HEAD_DIM, KEY_CHUNK), lambda i: (0, i, 0, 0)),
        pl.BlockSpec((tm, fw), row),
        pl.BlockSpec((tm, fw), row),
    ]
    in_specs = [
        pl.BlockSpec((tm, d), row),
        pl.BlockSpec((None, N_MOD, d), lambda i: ((i * tm) // seq, 0, 0)),
        pl.BlockSpec((1, d), const2),
        lay(d, ncols),
        lay(1, rq),
        lay(heads * QK_DIM, rq),
        lay(QK_DIM, 1),
        lay(1, rkv),
        lay(rkv, heads * QK_NOPE_DIM),
        lay(heads * V_HEAD_DIM, rkv),
        lay(1, LANES), lay(1, LANES), lay(1, LANES),
        pl.BlockSpec((tm, LANES), pos), pl.BlockSpec((tm, LANES), pos),
        pl.BlockSpec((QK_ROPE_DIM, tm), post), pl.BlockSpec((QK_ROPE_DIM, tm), post),
        pl.BlockSpec((ch, 2 * ch), const2),
    ]
    return pl.pallas_call(
        functools.partial(_mixer_in_kernel, heads=heads, rq=rq, rkv=rkv, fw=fw, groups=groups),
        grid=(rows // tm,),
        in_specs=in_specs,
        out_specs=out_specs,
        out_shape=out_shape,
        compiler_params=_cparams(("parallel",)),
        name="mixer_in",
    )(xa, mods, g, p["w_in"], p["q_lat_g"], p["w_uqt"], p["gq"], p["kv_lat_g"], p["w_uk"], p["w_uvt"],
      p["gk_n"], p["gk_r"], p["gk_rs"], p["rope_ck"], p["rope_sk"], p["rope_cqt"], p["rope_sqt"], p["dft_ch"])


DFT_BATCH = 16


def _pack_bf16_row_pair(even, odd):
    lo = lax.bitcast_convert_type(even.astype(BF16).astype(F32), U32) >> 16
    hi = lax.bitcast_convert_type(odd.astype(BF16).astype(F32), U32)
    return lo | hi


def _dft_a_kernel(zr_ref, zi_ref, fa_ref, tc_ref, ts_ref, o_ref, *, fw):
    n1 = fa_ref.shape[0] // 2
    fa = fa_ref[...]
    reps = fw // LANES
    for t2 in range(DFT_BATCH // 2):
        res = []
        for odd in (0, 1):
            t = 2 * t2 + odd
            cols = slice(t * fw, (t + 1) * fw)
            zst = jnp.concatenate([zr_ref[:, cols], zi_ref[:, cols]], axis=0)
            z1 = _dot(fa, zst)
            z1r, z1i = z1[:n1], z1[n1:]
            tc = jnp.concatenate([tc_ref[t]] * reps, axis=1)
            ts = jnp.concatenate([ts_ref[t]] * reps, axis=1)
            res.append((z1r * tc + z1i * ts, z1i * tc - z1r * ts))
        o_ref[0, :, t2, :] = _pack_bf16_row_pair(res[0][0], res[1][0])
        o_ref[1, :, t2, :] = _pack_bf16_row_pair(res[0][1], res[1][1])


def _dft_a(zr, zi, p, *, batch, n1, n2, fw):
    rows = zr.shape[0]
    cb = DFT_BATCH
    zr2 = zr.reshape(rows // n2, n2 * fw)
    zi2 = zi.reshape(rows // n2, n2 * fw)
    blk = pl.BlockSpec((n1, cb * fw), lambda b, j: (b, j))
    tw = pl.BlockSpec((cb, n1, LANES), lambda b, j: (j, 0, 0))
    return pl.pallas_call(
        functools.partial(_dft_a_kernel, fw=fw),
        grid=(batch, n2 // cb),
        in_specs=[blk, blk, pl.BlockSpec((2 * n1, 2 * n1), lambda b, j: (0, 0)), tw, tw],
        out_specs=pl.BlockSpec((None, 2, n1, cb // 2, fw), lambda b, j: (b, 0, 0, j, 0)),
        out_shape=jax.ShapeDtypeStruct((batch, 2, n1, n2 // 2, fw), U32),
        compiler_params=_cparams(("parallel", "parallel")),
        name="dft_rows_a",
    )(zr2, zi2, p["dft_a"], p["tw_c"], p["tw_s"])


def _dft_b_kernel(z_ref, fb_ref, o_ref):
    fb = fb_ref[...]
    fw = z_ref.shape[3]
    for t2 in range(DFT_BATCH // 2):
        even, odd = (_dot(fb, _unpairs(z_ref[:, 2 * t2 + h]).reshape(-1, fw)) for h in (0, 1))
        o_ref[:, t2, :] = _pack_bf16_row_pair(even, odd)


def _dft_b(z2, p, *, rows, batch, n1, n2, fw):
    cb = DFT_BATCH
    y = pl.pallas_call(
        _dft_b_kernel,
        grid=(batch, n1 // cb),
        in_specs=[
            pl.BlockSpec((None, 2, cb, n2 // 2, fw), lambda b, j: (b, 0, j, 0, 0)),
            pl.BlockSpec((n2, 2 * n2), lambda b, j: (0, 0)),
        ],
        out_specs=pl.BlockSpec((n2, cb // 2, fw), lambda b, j: (b, j, 0)),
        out_shape=jax.ShapeDtypeStruct((rows // n1, n1 // 2, fw), U32),
        compiler_params=_cparams(("parallel", "parallel")),
        name="dft_rows_b",
    )(z2, p["dft_b"])
    return y.reshape(rows // 2, fw)


def _attn_kernel(qt_ref, k_ref, kc_ref, vt_ref, vtc_ref, o_ref, s_ref, acc_ref, *, tk, unroll):
    tq = qt_ref.shape[1]
    per = tk // KEY_CHUNK
    n = k_ref.shape[0] // tk

    def scores(j, slot):
        s = _dot(k_ref[pl.ds(pl.multiple_of(j * tk, tk), tk), :], qt_ref[...])
        s_ref[slot] = s
        return jnp.max(s, axis=0, keepdims=True)

    def absorb(load_s, vts, cmax, m):
        m_new = jnp.maximum(m, cmax)
        alpha = jnp.exp2(m - m_new)
        vt = vts[0] if len(vts) == 1 else jnp.concatenate(vts, axis=1)
        vt = jnp.concatenate([vt, jnp.ones((BF16_ROWS, vt.shape[1]), BF16)], axis=0)
        for c in range(0, tq, MXU_COLS):
            cs = slice(c, c + MXU_COLS)
            p = jnp.exp2(load_s(cs) - m_new[:, cs])
            acc_ref[:, cs] = alpha[:, cs] * acc_ref[:, cs] + _dot(vt, p.astype(BF16))
        return m_new

    def absorb_chunk(j, slot, cmax, m):
        return absorb(lambda cs: s_ref[slot, :, cs], [vt_ref[j * per + c] for c in range(per)], cmax, m)

    acc_ref[...] = jnp.zeros_like(acc_ref)
    sc = _dot(kc_ref[...], qt_ref[...])
    cmax0 = scores(0, 0)
    m = absorb(lambda cs: sc[:, cs], [vtc_ref[0]], jnp.max(sc, axis=0, keepdims=True),
               jnp.full((1, tq), -1e30, F32))

    def group(i, carry):
        m, cmax = carry
        for u in range(unroll):
            j = i * unroll + u
            cmax_next = scores(jnp.minimum(j + 1, n - 1), (u + 1) % 2)
            m = absorb_chunk(j, u % 2, cmax, m)
            cmax = cmax_next
        return m, cmax

    lax.fori_loop(0, n // unroll, group, (m, cmax0))
    o_ref[...] = (acc_ref[:V_HEAD_DIM] / acc_ref[V_HEAD_DIM:V_HEAD_DIM + 1]).T.astype(BF16)


def _attention(qt, k, vt, *, batch, seq, ctx_len, heads):
    rows = k.shape[1]
    tq, tk = ATT_TQ, ATT_TK
    nq = seq // tq
    lat_rows = batch * seq
    return pl.pallas_call(
        functools.partial(_attn_kernel, tk=tk, unroll=math.gcd(ATT_UNROLL, seq // tk)),
        grid=(batch, heads, nq),
        in_specs=[
            pl.BlockSpec((None, QK_DIM, tq), lambda b, h, i: (h, 0, b * nq + i)),
            pl.BlockSpec((None, seq, QK_DIM), lambda b, h, i: (h, b, 0)),
            pl.BlockSpec((None, ctx_len, QK_DIM), lambda b, h, i: (h, lat_rows // ctx_len + b, 0)),
            pl.BlockSpec((None, seq // KEY_CHUNK, V_HEAD_DIM, KEY_CHUNK), lambda b, h, i: (h, b, 0, 0)),
            pl.BlockSpec((None, 1, V_HEAD_DIM, KEY_CHUNK), lambda b, h, i: (h, lat_rows // KEY_CHUNK + b, 0, 0)),
        ],
        out_specs=pl.BlockSpec((tq, V_HEAD_DIM), lambda b, h, i: (b * nq + i, h)),
        out_shape=jax.ShapeDtypeStruct((rows, heads * V_HEAD_DIM), BF16),
        scratch_shapes=[pltpu.VMEM((2, tk, tq), F32), pltpu.VMEM((V_HEAD_DIM + BF16_ROWS, tq), F32)],
        compiler_params=_cparams(("parallel", "parallel", "arbitrary")),
        name="attention",
    )(qt, k, k, vt, vt)


def _ctx_kernel(qt_ref, k_ref, vt_ref, zr_ref, zi_ref, fc_ref, o_in, y_in, o_ref, y_ref, *, heads):
    del o_in, y_in
    for hd in range(heads):
        s = _dot(k_ref[hd], qt_ref[hd])
        p = jnp.exp2(s - jnp.max(s, axis=0, keepdims=True))
        l = jnp.sum(p, axis=0, keepdims=True)
        o = _dot(vt_ref[hd, 0], p.astype(BF16)) / l
        o_ref[:, hd * V_HEAD_DIM:(hd + 1) * V_HEAD_DIM] = o.T.astype(BF16)
    zst = jnp.concatenate([zr_ref[...], zi_ref[...]], axis=0)
    y_ref[...] = _pairs(_dot(fc_ref[...], zst).astype(BF16))


def _ctx_mixer(qt, k, vt, zr, zi, o, y, p, *, batch, lat_rows, ctx_len, heads, fw):
    c0 = lat_rows // ctx_len
    return pl.pallas_call(
        functools.partial(_ctx_kernel, heads=heads),
        grid=(batch,),
        in_specs=[
            pl.BlockSpec((heads, QK_DIM, ctx_len), lambda b: (0, 0, c0 + b)),
            pl.BlockSpec((heads, ctx_len, QK_DIM), lambda b: (0, c0 + b, 0)),
            pl.BlockSpec((heads, 1, V_HEAD_DIM, KEY_CHUNK), lambda b: (0, c0 + b, 0, 0)),
            pl.BlockSpec((ctx_len, fw), lambda b: (c0 + b, 0)),
            pl.BlockSpec((ctx_len, fw), lambda b: (c0 + b, 0)),
            pl.BlockSpec((ctx_len, 2 * ctx_len), lambda b: (0, 0)),
            pl.BlockSpec(memory_space=pl.ANY),
            pl.BlockSpec(memory_space=pl.ANY),
        ],
        out_specs=[
            pl.BlockSpec((ctx_len, heads * V_HEAD_DIM), lambda b: (c0 + b, 0)),
            pl.BlockSpec((ctx_len // 2, fw), lambda b: (c0 + b, 0)),
        ],
        out_shape=[jax.ShapeDtypeStruct(o.shape, o.dtype), jax.ShapeDtypeStruct(y.shape, y.dtype)],
        input_output_aliases={6: 0, 7: 1},
        compiler_params=_cparams(("parallel",)),
        name="ctx_mixer",
    )(qt, k, vt, zr, zi, p["dft_ctx"], o, y)


def _mixer_out_kernel(x_ref, mod_ref, a_ref, y_ref, wf_ref, wo_ref, o_ref, *, groups):
    ch = y_ref.shape[1] // groups
    parts = [a_ref[...]]
    for gi in range(groups):
        y = _unpairs(y_ref[:, gi * ch:(gi + 1) * ch])
        parts.append(_dot(y, wf_ref[gi]).astype(BF16))
    cat = jnp.concatenate(parts, axis=1)
    o_ref[...] = x_ref[...] + mod_ref[5:6, :] * _dot(cat, wo_ref[...])


def _mixer_out(xa, mods, o, y, p, layer, *, rows, seq, groups):
    d = xa.shape[1]
    tm = ROW_TILE
    aw, fw = o.shape[1], y.shape[1]
    ch = fw // groups
    row = lambda i: (i, 0)
    return pl.pallas_call(
        functools.partial(_mixer_out_kernel, groups=groups),
        grid=(rows // tm,),
        in_specs=[
            pl.BlockSpec((tm, d), row),
            pl.BlockSpec((None, N_MOD, d), lambda i: ((i * tm) // seq, 0, 0)),
            pl.BlockSpec((tm, aw), row),
            pl.BlockSpec((tm // 2, fw), row),
            pl.BlockSpec((None, groups, ch, ch), lambda i: (layer, 0, 0, 0), pipeline_mode=pl.Buffered(1)),
            pl.BlockSpec((None, aw + fw, d), lambda i: (layer, 0, 0), pipeline_mode=pl.Buffered(1)),
        ],
        out_specs=pl.BlockSpec((tm, d), row),
        out_shape=jax.ShapeDtypeStruct((rows, d), F32),
        compiler_params=_cparams(("parallel",)),
        name="mixer_out",
    )(xa, mods, o, y, p["w_f"], p["w_o"])


def _dft_parts(n):
    idx = np.arange(n, dtype=np.int64)
    ang = 2.0 * np.pi * ((idx[:, None] * idx[None, :]) % n).astype(np.float64) / n
    s = 1.0 / math.sqrt(n)
    return np.cos(ang) * s, np.sin(ang) * s


def _tables(seq, ctx_len, ch, n1, n2, tm):
    t = {}
    c, s = _dft_parts(ch)
    t["dft_ch"] = jnp.asarray(np.concatenate([c, -s], axis=1), BF16)
    c, s = _dft_parts(n1)
    t["dft_a"] = jnp.asarray(np.block([[c, s], [-s, c]]), BF16)
    c, s = _dft_parts(n2)
    t["dft_b"] = jnp.asarray(np.concatenate([c, s], axis=1), BF16)
    c, s = _dft_parts(ctx_len)
    t["dft_ctx"] = jnp.asarray(np.concatenate([c, s], axis=1), BF16)
    i2 = np.arange(n2, dtype=np.int64)[:, None]
    k1 = np.arange(n1, dtype=np.int64)[None, :]
    ang = 2.0 * np.pi * ((i2 * k1) % seq).astype(np.float64) / seq
    t["tw_c"] = jnp.asarray(np.broadcast_to(np.cos(ang)[:, :, None], (n2, n1, LANES)), F32)
    t["tw_s"] = jnp.asarray(np.broadcast_to(np.sin(ang)[:, :, None], (n2, n1, LANES)), F32)

    rows = seq // GRID_W
    row = jnp.broadcast_to(jnp.arange(rows, dtype=F32)[:, None], (rows, GRID_W)).reshape(-1)
    col = jnp.broadcast_to(jnp.arange(GRID_W, dtype=F32)[None, :], (rows, GRID_W)).reshape(-1)
    inv_freq = ROPE_THETA ** (-jnp.arange(AXIS_FREQS, dtype=F32) / AXIS_FREQS)
    ang_r = row[:, None] * inv_freq
    ang_c = col[:, None] * inv_freq
    cr, sr, cc, sc = jnp.cos(ang_r), jnp.sin(ang_r), jnp.cos(ang_c), jnp.sin(ang_c)
    cos = jnp.concatenate([cr, cr, cc, cc], axis=1)
    sin = jnp.concatenate([-sr, sr, -sc, sc], axis=1)
    cos = jnp.concatenate([cos, jnp.ones((tm, QK_ROPE_DIM), F32)], axis=0)
    sin = jnp.concatenate([sin, jnp.zeros((tm, QK_ROPE_DIM), F32)], axis=0)
    pad = ((0, 0), (0, LANES - QK_ROPE_DIM))
    t["rope_ck"] = jnp.pad(cos, pad)
    t["rope_sk"] = jnp.pad(sin, pad)
    t["rope_cqt"] = cos.T
    t["rope_sqt"] = sin.T
    return t


def _swap_perm():
    a = AXIS_FREQS
    return np.concatenate([np.arange(a, 2 * a), np.arange(0, a), np.arange(3 * a, 4 * a), np.arange(2 * a, 3 * a)])


def kernel(x, c, ctx, c_ctx, w_ada, b_ada, norm_g, ffn1_w_gate, ffn1_w_up, ffn1_w_down, ffn2_w_gate, ffn2_w_up,
           ffn2_w_down, w_in, q_lat_g, w_uq, kv_lat_g, w_ukv, q_norm_g, k_norm_g, w_fourier, w_o):
    batch, seq, d = x.shape
    ctx_len = ctx.shape[1]
    depth = w_ada.shape[0]
    rq = q_lat_g.shape[1]
    rkv = kv_lat_g.shape[1]
    heads = w_uq.shape[2] // QK_DIM
    groups = w_fourier.shape[1]
    ch = w_fourier.shape[2]
    fw = groups * ch
    tm = ROW_TILE
    lat_rows = batch * seq
    ctx_rows = batch * ctx_len
    rows = lat_rows + ctx_rows
    n1 = 1 << ((seq.bit_length() - 1) // 2)
    n2 = seq // n1
    assert n1 * n2 == seq and n1 % DFT_BATCH == 0 and n2 % DFT_BATCH == 0
    assert ctx_len == KEY_CHUNK and seq % tm == 0 and ctx_rows % tm == 0 and ctx_rows <= seq
    assert rows % n1 == 0 and rows % n2 == 0 and seq % ATT_TQ == 0 and seq % (2 * ATT_TK) == 0
    assert batch + 1 <= COND_ROWS and w_in.shape[2] == rq + rkv + QK_ROPE_DIM + fw
    assert w_ukv.shape[2] == heads * (QK_NOPE_DIM + V_HEAD_DIM)

    perm = _swap_perm()
    k0 = rq + rkv
    zpad = jnp.zeros((depth, d, LANES - QK_ROPE_DIM), w_in.dtype)
    w_rope = w_in[:, :, k0:k0 + QK_ROPE_DIM]
    w_in_ext = jnp.concatenate(
        [w_in[:, :, :k0], w_in[:, :, k0 + QK_ROPE_DIM:], w_rope, zpad, w_rope[:, :, perm], zpad], axis=2)
    w_ukv4 = w_ukv.reshape(depth, rkv, heads, QK_NOPE_DIM + V_HEAD_DIM)
    gpad = ((0, 0), (0, 0), (0, LANES - QK_ROPE_DIM))
    g_rope = k_norm_g[:, None, QK_NOPE_DIM:]
    p = _tables(seq, ctx_len, ch, n1, n2, tm)
    p.update(
        w_in=w_in_ext.astype(BF16),
        q_lat_g=q_lat_g[:, None, :],
        w_uqt=jnp.swapaxes(w_uq, 1, 2).astype(BF16),
        gq=q_norm_g[:, :, None],
        kv_lat_g=kv_lat_g[:, None, :],
        w_uk=w_ukv4[..., :QK_NOPE_DIM].reshape(depth, rkv, heads * QK_NOPE_DIM).astype(BF16),
        w_uvt=jnp.swapaxes(w_ukv4[..., QK_NOPE_DIM:].reshape(depth, rkv, heads * V_HEAD_DIM), 1, 2).astype(BF16),
        gk_n=k_norm_g[:, None, :QK_NOPE_DIM],
        gk_r=jnp.pad(g_rope, gpad),
        gk_rs=jnp.pad(g_rope[:, :, perm], gpad),
        w_f=w_fourier.astype(BF16),
        w_o=w_o.astype(BF16),
    )
    ffn1 = tuple(w.astype(BF16) for w in (ffn1_w_gate, ffn1_w_up, ffn1_w_down))
    ffn2 = tuple(w.astype(BF16) for w in (ffn2_w_gate, ffn2_w_up, ffn2_w_down))

    cond = jnp.concatenate([c, c_ctx[None, :], jnp.zeros((COND_ROWS - batch - 1, d), c.dtype)], axis=0)
    mods_all = _adaln(cond, w_ada, b_ada).reshape(depth, COND_ROWS, N_MOD, d)

    xa = jnp.concatenate([x.reshape(lat_rows, d), ctx.reshape(ctx_rows, d)], axis=0)
    dims = (heads, rq, rkv, fw, groups)
    for layer in range(depth):
        last = layer == depth - 1
        mods = mods_all[layer]
        g = norm_g[layer]
        xa = _ffn(xa, mods, g[0:1], *ffn1, layer, mod0=0, rows=rows, seq=seq)
        qt, k, vt, zr, zi = _mixer_in(xa, mods, g[1:2], p, layer, seq=seq, lat_rows=lat_rows, dims=dims)
        z2 = _dft_a(zr, zi, p, batch=batch, n1=n1, n2=n2, fw=fw)
        y = _dft_b(z2, p, rows=rows, batch=batch, n1=n1, n2=n2, fw=fw)
        o = _attention(qt, k, vt, batch=batch, seq=seq, ctx_len=ctx_len, heads=heads)
        if not last:
            o, y = _ctx_mixer(qt, k, vt, zr, zi, o, y, p, batch=batch, lat_rows=lat_rows, ctx_len=ctx_len,
                              heads=heads, fw=fw)
        out_rows = lat_rows if last else rows
        xa = _mixer_out(xa, mods, o, y, p, layer, rows=out_rows, seq=seq, groups=groups)
        xa = _ffn(xa, mods, g[2:3], *ffn2, layer, mod0=6, rows=out_rows, seq=seq)
    return xa.reshape(batch, seq, d)
```

```python
import functools
import math

import jax
import jax.numpy as jnp
import numpy as np
from jax import lax
from jax.experimental import pallas as pl
from jax.experimental.pallas import tpu as pltpu

GRID_W = 64
V_HEAD_DIM = 128
QK_NOPE_DIM = 128
QK_ROPE_DIM = 64
QK_DIM = QK_NOPE_DIM + QK_ROPE_DIM
N_MOD = 9
AXIS_FREQS = QK_ROPE_DIM // 4
ROPE_THETA = 10000.0
SM_SCALE = QK_DIM ** -0.5
RMS_EPS = 1e-6
LOG2E = 1.4426950408889634

LANES = 128
MXU_COLS = 256
BF16_ROWS = 16
COND_ROWS = 8
VMEM_LIMIT = 60 * 1024 * 1024

ROW_TILE = 512
KEY_CHUNK = 256
FFN_TILE = 1024
ADA_TILE = 1024
ATT_TQ = 1024
ATT_TK = 512
ATT_UNROLL = 16

BF16 = jnp.bfloat16
F32 = jnp.float32
U32 = jnp.uint32


def _pairs(x):
    return pltpu.bitcast(x, U32)


def _unpairs(u):
    return pltpu.bitcast(u, BF16)


def _dot(a, b):
    return jnp.dot(a, b, preferred_element_type=F32)


def _dot_nt(a, b):
    return lax.dot_general(a, b, (((1,), (1,)), ((), ())), preferred_element_type=F32)


def _cparams(sem):
    return pltpu.CompilerParams(dimension_semantics=sem, vmem_limit_bytes=VMEM_LIMIT)


def _norm_mod(x, g, shift, scale):
    y = x * lax.rsqrt(jnp.mean(x * x, axis=-1, keepdims=True) + RMS_EPS) * g
    return y * (1.0 + scale) + shift


def _adaln_kernel(c_ref, w_ref, b_ref, o_ref):
    c = c_ref[...]
    s = (c * jax.nn.sigmoid(c)).astype(BF16)
    o_ref[...] = _dot(s, w_ref[...].astype(BF16)) + b_ref[...]


def _adaln(cond, w_ada, b_ada):
    depth, d, n = w_ada.shape
    tn = math.gcd(ADA_TILE, d)
    return pl.pallas_call(
        _adaln_kernel,
        grid=(depth, n // tn),
        in_specs=[
            pl.BlockSpec((COND_ROWS, d), lambda l, j: (0, 0)),
            pl.BlockSpec((None, d, tn), lambda l, j: (l, 0, j)),
            pl.BlockSpec((None, 1, tn), lambda l, j: (l, 0, j)),
        ],
        out_specs=pl.BlockSpec((None, COND_ROWS, tn), lambda l, j: (l, 0, j)),
        out_shape=jax.ShapeDtypeStruct((depth, COND_ROWS, n), F32),
        compiler_params=_cparams(("arbitrary", "arbitrary")),
        name="adaln",
    )(cond, w_ada, b_ada.reshape(depth, 1, n))


def _ffn_kernel(x_ref, mod_ref, g_ref, wg_ref, wu_ref, wd_ref, wgt_ref, wut_ref, wdt_ref, *rest, mod0, n_wide):
    o_ref, h_ref, acc_ref = rest[-3:]
    f = pl.program_id(1)

    def block(wg, wu, wd):
        h = h_ref[...]
        gate = _dot(h, wg[...])
        up = _dot(h, wu[...])
        a = (gate * jax.nn.sigmoid(gate) * up).astype(BF16)
        return _dot(a, wd[...])

    @pl.when(f == 0)
    def _():
        h = _norm_mod(x_ref[...], g_ref[...], mod_ref[mod0:mod0 + 1, :], mod_ref[mod0 + 1:mod0 + 2, :])
        h_ref[...] = h.astype(BF16)
        acc_ref[...] = block(wgt_ref, wut_ref, wdt_ref)

    @pl.when((f > 0) & (f < n_wide))
    def _():
        acc_ref[...] += block(wg_ref, wu_ref, wd_ref)

    @pl.when(f == n_wide)
    def _():
        total = acc_ref[...] + block(wg_ref, wu_ref, wd_ref)
        o_ref[...] = x_ref[...] + (0.5 * mod_ref[mod0 + 2:mod0 + 3, :]) * total


def _ffn(xa, mods, g, wg, wu, wd, layer, *, mod0, rows, seq, out_rows=None, tile0=0, into=None):
    d = xa.shape[1]
    d_ff = wg.shape[2]
    tm, tf = ROW_TILE, FFN_TILE
    out_rows = rows if out_rows is None else out_rows
    n_wide = (d_ff - 1) // tf
    tail = d_ff - n_wide * tf
    assert n_wide >= 1 and tail % LANES == 0 and (n_wide * tf) % tail == 0
    tail_blk = (n_wide * tf) // tail
    row = lambda i, f: (i, 0)
    wide = lambda i, f: (layer, 0, jnp.maximum(f - 1, 0))
    wide_t = lambda i, f: (layer, jnp.maximum(f - 1, 0), 0)
    once = pl.Buffered(1)
    kept = [] if into is None else [into]
    return pl.pallas_call(
        functools.partial(_ffn_kernel, mod0=mod0, n_wide=n_wide),
        grid=(rows // tm, n_wide + 1),
        in_specs=[
            pl.BlockSpec((tm, d), row),
            pl.BlockSpec((None, N_MOD, d), lambda i, f: (((i + tile0) * tm) // seq, 0, 0)),
            pl.BlockSpec((1, d), lambda i, f: (0, 0)),
            pl.BlockSpec((None, d, tf), wide),
            pl.BlockSpec((None, d, tf), wide),
            pl.BlockSpec((None, tf, d), wide_t),
            pl.BlockSpec((None, d, tail), lambda i, f: (layer, 0, tail_blk), pipeline_mode=once),
            pl.BlockSpec((None, d, tail), lambda i, f: (layer, 0, tail_blk), pipeline_mode=once),
            pl.BlockSpec((None, tail, d), lambda i, f: (layer, tail_blk, 0), pipeline_mode=once),
        ] + [pl.BlockSpec(memory_space=pl.ANY)] * len(kept),
        out_specs=pl.BlockSpec((tm, d), lambda i, f: (i + tile0, 0)),
        out_shape=jax.ShapeDtypeStruct((out_rows, d), F32),
        input_output_aliases={9: 0} if kept else {},
        scratch_shapes=[pltpu.VMEM((tm, d), BF16), pltpu.VMEM((tm, d), F32)],
        compiler_params=_cparams(("parallel", "arbitrary")),
        name="ffn",
    )(xa, mods, g, wg, wu, wd, wg, wu, wd, *kept)


def _mixer_in_kernel(x_ref, mod_ref, g_ref, win_ref, gql_ref, wuqt_ref, gq_ref, gkvl_ref, wuk_ref, wuvt_ref,
                     gkn_ref, gkr_ref, gkrs_ref, ck_ref, sk_ref, cq_ref, sq_ref, dft_ref,
                     qt_ref, k_ref, vt_ref, zr_ref, zi_ref, *, heads, rq, rkv, fw, groups):
    tm = x_ref.shape[0]
    h = _norm_mod(x_ref[...], g_ref[...], mod_ref[3:4, :], mod_ref[4:5, :]).astype(BF16)
    proj = _dot(h, win_ref[...])
    f0 = rq + rkv
    r0 = f0 + fw

    qlat = proj[:, :rq]
    qn = (qlat * lax.rsqrt(jnp.mean(qlat * qlat, axis=-1, keepdims=True) + RMS_EPS) * gql_ref[...]).astype(BF16)
    qt = _dot_nt(wuqt_ref[...], qn)
    gq = gq_ref[...]
    cq = cq_ref[...]
    sq = sq_ref[...]
    a = AXIS_FREQS
    for hd in range(heads):
        blk = qt[hd * QK_DIM:(hd + 1) * QK_DIM]
        r = lax.rsqrt(jnp.mean(blk * blk, axis=0, keepdims=True) + RMS_EPS) * (SM_SCALE * LOG2E)
        blk = blk * r * gq
        rp = blk[QK_NOPE_DIM:]
        sw = jnp.concatenate([rp[a:2 * a], rp[:a], rp[3 * a:], rp[2 * a:3 * a]], axis=0)
        qt_ref[hd, :QK_NOPE_DIM, :] = blk[:QK_NOPE_DIM].astype(BF16)
        qt_ref[hd, QK_NOPE_DIM:, :] = (rp * cq + sw * sq).astype(BF16)

    kvlat = proj[:, rq:f0]
    kvn = (kvlat * lax.rsqrt(jnp.mean(kvlat * kvlat, axis=-1, keepdims=True) + RMS_EPS) * gkvl_ref[...]).astype(BF16)
    knope = _dot(kvn, wuk_ref[...])
    vt = _dot_nt(wuvt_ref[...], kvn)
    for c in range(tm // KEY_CHUNK):
        piece = vt[:, c * KEY_CHUNK:(c + 1) * KEY_CHUNK].astype(BF16)
        vt_ref[:, c] = piece.reshape(heads, V_HEAD_DIM, KEY_CHUNK)
    kr = proj[:, r0:r0 + LANES]
    krs = proj[:, r0 + LANES:r0 + 2 * LANES]
    krr = kr * gkr_ref[...] * ck_ref[...] + krs * gkrs_ref[...] * sk_ref[...]
    ss_r = jnp.sum(kr * kr, axis=-1, keepdims=True)
    gkn = gkn_ref[...]
    for hd in range(heads):
        kn = knope[:, hd * QK_NOPE_DIM:(hd + 1) * QK_NOPE_DIM]
        ss = jnp.sum(kn * kn, axis=-1, keepdims=True) + ss_r
        r = lax.rsqrt(ss * (1.0 / QK_DIM) + RMS_EPS)
        k_ref[hd, :, :QK_NOPE_DIM] = (kn * r * gkn).astype(BF16)
        k_ref[hd, :, QK_NOPE_DIM:] = (krr * r)[:, :QK_ROPE_DIM].astype(BF16)

    ch = fw // groups
    fb = proj[:, f0:r0].astype(BF16)
    dft = dft_ref[...]
    for gi in range(groups):
        uw = _dot(fb[:, gi * ch:(gi + 1) * ch], dft)
        zr_ref[:, gi * ch:(gi + 1) * ch] = uw[:, :ch].astype(BF16)
        zi_ref[:, gi * ch:(gi + 1) * ch] = uw[:, ch:].astype(BF16)


def _mixer_in(xa, mods, g, p, layer, *, seq, lat_rows, dims):
    rows, d = xa.shape
    heads, rq, rkv, fw, groups = dims
    tm = ROW_TILE
    ncols = p["w_in"].shape[2]
    n_lat = lat_rows // tm
    per_seq = seq // tm
    const2 = lambda i: (0, 0)
    row = lambda i: (i, 0)
    pos = lambda i: (jnp.where(i < n_lat, i % per_seq, per_seq), 0)
    post = lambda i: (0, jnp.where(i < n_lat, i % per_seq, per_seq))
    lay = lambda *shape: pl.BlockSpec((None,) + shape, lambda i: (layer,) + (0,) * len(shape),
                                      pipeline_mode=pl.Buffered(1))
    ch = fw // groups
    out_shape = [
        jax.ShapeDtypeStruct((heads, QK_DIM, rows), BF16),
        jax.ShapeDtypeStruct((heads, rows, QK_DIM), BF16),
        jax.ShapeDtypeStruct((heads, rows // KEY_CHUNK, V_HEAD_DIM, KEY_CHUNK), BF16),
        jax.ShapeDtypeStruct((rows, fw), BF16),
        jax.ShapeDtypeStruct((rows, fw), BF16),
    ]
    out_specs = [
        pl.BlockSpec((heads, QK_DIM, tm), lambda i: (0, 0, i)),
        pl.BlockSpec((heads, tm, QK_DIM), lambda i: (0, i, 0)),
        pl.BlockSpec((heads, tm // KEY_CHUNK, V_HEAD_DIM, KEY_CHUNK), lambda i: (0, i, 0, 0)),
        pl.BlockSpec((tm, fw), row),
        pl.BlockSpec((tm, fw), row),
    ]
    in_specs = [
        pl.BlockSpec((tm, d), row),
        pl.BlockSpec((None, N_MOD, d), lambda i: ((i * tm) // seq, 0, 0)),
        pl.BlockSpec((1, d), const2),
        lay(d, ncols),
        lay(1, rq),
        lay(heads * QK_DIM, rq),
        lay(QK_DIM, 1),
        lay(1, rkv),
        lay(rkv, heads * QK_NOPE_DIM),
        lay(heads * V_HEAD_DIM, rkv),
        lay(1, LANES), lay(1, LANES), lay(1, LANES),
        pl.BlockSpec((tm, LANES), pos), pl.BlockSpec((tm, LANES), pos),
        pl.BlockSpec((QK_ROPE_DIM, tm), post), pl.BlockSpec((QK_ROPE_DIM, tm), post),
        pl.BlockSpec((ch, 2 * ch), const2),
    ]
    return pl.pallas_call(
        functools.partial(_mixer_in_kernel, heads=heads, rq=rq, rkv=rkv, fw=fw, groups=groups),
        grid=(rows // tm,),
        in_specs=in_specs,
        out_specs=out_specs,
        out_shape=out_shape,
        compiler_params=_cparams(("parallel",)),
        name="mixer_in",
    )(xa, mods, g, p["w_in"], p["q_lat_g"], p["w_uqt"], p["gq"], p["kv_lat_g"], p["w_uk"], p["w_uvt"],
      p["gk_n"], p["gk_r"], p["gk_rs"], p["rope_ck"], p["rope_sk"], p["rope_cqt"], p["rope_sqt"], p["dft_ch"])


DFT_BATCH = 16


def _pack_bf16_row_pair(even, odd):
    lo = lax.bitcast_convert_type(even.astype(BF16).astype(F32), U32) >> 16
    hi = lax.bitcast_convert_type(odd.astype(BF16).astype(F32), U32)
    return lo | hi


def _dft_a_kernel(zr_ref, zi_ref, fa_ref, tc_ref, ts_ref, o_ref, *, fw):
    n1 = fa_ref.shape[0] // 2
    fa = fa_ref[...]
    reps = fw // LANES
    for t2 in range(DFT_BATCH // 2):
        res = []
        for odd in (0, 1):
            t = 2 * t2 + odd
            cols = slice(t * fw, (t + 1) * fw)
            zst = jnp.concatenate([zr_ref[:, cols], zi_ref[:, cols]], axis=0)
            z1 = _dot(fa, zst)
            z1r, z1i = z1[:n1], z1[n1:]
            tc = jnp.concatenate([tc_ref[t]] * reps, axis=1)
            ts = jnp.concatenate([ts_ref[t]] * reps, axis=1)
            res.append((z1r * tc + z1i * ts, z1i * tc - z1r * ts))
        o_ref[0, :, t2, :] = _pack_bf16_row_pair(res[0][0], res[1][0])
        o_ref[1, :, t2, :] = _pack_bf16_row_pair(res[0][1], res[1][1])


def _dft_a(zr, zi, p, *, batch, n1, n2, fw):
    rows = zr.shape[0]
    cb = DFT_BATCH
    zr2 = zr.reshape(rows // n2, n2 * fw)
    zi2 = zi.reshape(rows // n2, n2 * fw)
    blk = pl.BlockSpec((n1, cb * fw), lambda b, j: (b, j))
    tw = pl.BlockSpec((cb, n1, LANES), lambda b, j: (j, 0, 0))
    return pl.pallas_call(
        functools.partial(_dft_a_kernel, fw=fw),
        grid=(batch, n2 // cb),
        in_specs=[blk, blk, pl.BlockSpec((2 * n1, 2 * n1), lambda b, j: (0, 0)), tw, tw],
        out_specs=pl.BlockSpec((None, 2, n1, cb // 2, fw), lambda b, j: (b, 0, 0, j, 0)),
        out_shape=jax.ShapeDtypeStruct((batch, 2, n1, n2 // 2, fw), U32),
        compiler_params=_cparams(("parallel", "parallel")),
        name="dft_rows_a",
    )(zr2, zi2, p["dft_a"], p["tw_c"], p["tw_s"])


def _dft_b_kernel(z_ref, fb_ref, o_ref):
    fb = fb_ref[...]
    fw = z_ref.shape[3]
    for t2 in range(DFT_BATCH // 2):
        even, odd = (_dot(fb, _unpairs(z_ref[:, 2 * t2 + h]).reshape(-1, fw)) for h in (0, 1))
        o_ref[:, t2, :] = _pack_bf16_row_pair(even, odd)


def _dft_b(z2, p, *, rows, batch, n1, n2, fw):
    cb = DFT_BATCH
    y = pl.pallas_call(
        _dft_b_kernel,
        grid=(batch, n1 // cb),
        in_specs=[
            pl.BlockSpec((None, 2, cb, n2 // 2, fw), lambda b, j: (b, 0, j, 0, 0)),
            pl.BlockSpec((n2, 2 * n2), lambda b, j: (0, 0)),
        ],
        out_specs=pl.BlockSpec((n2, cb // 2, fw), lambda b, j: (b, j, 0)),
        out_shape=jax.ShapeDtypeStruct((rows // n1, n1 // 2, fw), U32),
        compiler_params=_cparams(("parallel", "parallel")),
        name="dft_rows_b",
    )(z2, p["dft_b"])
    return y.reshape(rows // 2, fw)


def _attn_kernel(qt_ref, k_ref, kc_ref, vt_ref, vtc_ref, o_ref, s_ref, acc_ref, *, tk, unroll):
    tq = qt_ref.shape[1]
    per = tk // KEY_CHUNK
    n = k_ref.shape[0] // tk

    def scores(j, slot):
        s = _dot(k_ref[pl.ds(pl.multiple_of(j * tk, tk), tk), :], qt_ref[...])
        s_ref[slot] = s
        return jnp.max(s, axis=0, keepdims=True)

    def absorb(load_s, vts, cmax, m):
        m_new = jnp.maximum(m, cmax)
        alpha = jnp.exp2(m - m_new)
        vt = vts[0] if len(vts) == 1 else jnp.concatenate(vts, axis=1)
        vt = jnp.concatenate([vt, jnp.ones((BF16_ROWS, vt.shape[1]), BF16)], axis=0)
        for c in range(0, tq, MXU_COLS):
            cs = slice(c, c + MXU_COLS)
            p = jnp.exp2(load_s(cs) - m_new[:, cs])
            acc_ref[:, cs] = alpha[:, cs] * acc_ref[:, cs] + _dot(vt, p.astype(BF16))
        return m_new

    def absorb_chunk(j, slot, cmax, m):
        return absorb(lambda cs: s_ref[slot, :, cs], [vt_ref[j * per + c] for c in range(per)], cmax, m)

    acc_ref[...] = jnp.zeros_like(acc_ref)
    sc = _dot(kc_ref[...], qt_ref[...])
    cmax0 = scores(0, 0)
    m = absorb(lambda cs: sc[:, cs], [vtc_ref[0]], jnp.max(sc, axis=0, keepdims=True),
               jnp.full((1, tq), -1e30, F32))

    def group(i, carry):
        m, cmax = carry
        for u in range(unroll):
            j = i * unroll + u
            cmax_next = scores(jnp.minimum(j + 1, n - 1), (u + 1) % 2)
            m = absorb_chunk(j, u % 2, cmax, m)
            cmax = cmax_next
        return m, cmax

    lax.fori_loop(0, n // unroll, group, (m, cmax0))
    o_ref[...] = (acc_ref[:V_HEAD_DIM] / acc_ref[V_HEAD_DIM:V_HEAD_DIM + 1]).T.astype(BF16)


def _attention(qt, k, vt, *, batch, seq, ctx_len, heads):
    rows = k.shape[1]
    tq, tk = ATT_TQ, ATT_TK
    nq = seq // tq
    lat_rows = batch * seq
    return pl.pallas_call(
        functools.partial(_attn_kernel, tk=tk, unroll=math.gcd(ATT_UNROLL, seq // tk)),
        grid=(batch, heads, nq),
        in_specs=[
            pl.BlockSpec((None, QK_DIM, tq), lambda b, h, i: (h, 0, b * nq + i)),
            pl.BlockSpec((None, seq, QK_DIM), lambda b, h, i: (h, b, 0)),
            pl.BlockSpec((None, ctx_len, QK_DIM), lambda b, h, i: (h, lat_rows // ctx_len + b, 0)),
            pl.BlockSpec((None, seq // KEY_CHUNK, V_HEAD_DIM, KEY_CHUNK), lambda b, h, i: (h, b, 0, 0)),
            pl.BlockSpec((None, 1, V_HEAD_DIM, KEY_CHUNK), lambda b, h, i: (h, lat_rows // KEY_CHUNK + b, 0, 0)),
        ],
        out_specs=pl.BlockSpec((tq, V_HEAD_DIM), lambda b, h, i: (b * nq + i, h)),
        out_shape=jax.ShapeDtypeStruct((rows, heads * V_HEAD_DIM), BF16),
        scratch_shapes=[pltpu.VMEM((2, tk, tq), F32), pltpu.VMEM((V_HEAD_DIM + BF16_ROWS, tq), F32)],
        compiler_params=_cparams(("parallel", "parallel", "arbitrary")),
        name="attention",
    )(qt, k, k, vt, vt)


def _ctx_kernel(qt_ref, k_ref, vt_ref, zr_ref, zi_ref, fc_ref, o_in, y_in, o_ref, y_ref, *, heads):
    del o_in, y_in
    for hd in range(heads):
        s = _dot(k_ref[hd], qt_ref[hd])
        p = jnp.exp2(s - jnp.max(s, axis=0, keepdims=True))
        l = jnp.sum(p, axis=0, keepdims=True)
        o = _dot(vt_ref[hd, 0], p.astype(BF16)) / l
        o_ref[:, hd * V_HEAD_DIM:(hd + 1) * V_HEAD_DIM] = o.T.astype(BF16)
    zst = jnp.concatenate([zr_ref[...], zi_ref[...]], axis=0)
    y_ref[...] = _pairs(_dot(fc_ref[...], zst).astype(BF16))


def _ctx_mixer(qt, k, vt, zr, zi, o, y, p, *, batch, lat_rows, ctx_len, heads, fw):
    c0 = lat_rows // ctx_len
    return pl.pallas_call(
        functools.partial(_ctx_kernel, heads=heads),
        grid=(batch,),
        in_specs=[
            pl.BlockSpec((heads, QK_DIM, ctx_len), lambda b: (0, 0, c0 + b)),
            pl.BlockSpec((heads, ctx_len, QK_DIM), lambda b: (0, c0 + b, 0)),
            pl.BlockSpec((heads, 1, V_HEAD_DIM, KEY_CHUNK), lambda b: (0, c0 + b, 0, 0)),
            pl.BlockSpec((ctx_len, fw), lambda b: (c0 + b, 0)),
            pl.BlockSpec((ctx_len, fw), lambda b: (c0 + b, 0)),
            pl.BlockSpec((ctx_len, 2 * ctx_len), lambda b: (0, 0)),
            pl.BlockSpec(memory_space=pl.ANY),
            pl.BlockSpec(memory_space=pl.ANY),
        ],
        out_specs=[
            pl.BlockSpec((ctx_len, heads * V_HEAD_DIM), lambda b: (c0 + b, 0)),
            pl.BlockSpec((ctx_len // 2, fw), lambda b: (c0 + b, 0)),
        ],
        out_shape=[jax.ShapeDtypeStruct(o.shape, o.dtype), jax.ShapeDtypeStruct(y.shape, y.dtype)],
        input_output_aliases={6: 0, 7: 1},
        compiler_params=_cparams(("parallel",)),
        name="ctx_mixer",
    )(qt, k, vt, zr, zi, p["dft_ctx"], o, y)


def _mixer_out_kernel(x_ref, mod_ref, a_ref, y_ref, wf_ref, wo_ref, o_ref, *, groups):
    ch = y_ref.shape[1] // groups
    parts = [a_ref[...]]
    for gi in range(groups):
        y = _unpairs(y_ref[:, gi * ch:(gi + 1) * ch])
        parts.append(_dot(y, wf_ref[gi]).astype(BF16))
    cat = jnp.concatenate(parts, axis=1)
    o_ref[...] = x_ref[...] + mod_ref[5:6, :] * _dot(cat, wo_ref[...])


def _mixer_out(xa, mods, o, y, p, layer, *, rows, seq, groups):
    d = xa.shape[1]
    tm = ROW_TILE
    aw, fw = o.shape[1], y.shape[1]
    ch = fw // groups
    row = lambda i: (i, 0)
    return pl.pallas_call(
        functools.partial(_mixer_out_kernel, groups=groups),
        grid=(rows // tm,),
        in_specs=[
            pl.BlockSpec((tm, d), row),
            pl.BlockSpec((None, N_MOD, d), lambda i: ((i * tm) // seq, 0, 0)),
            pl.BlockSpec((tm, aw), row),
            pl.BlockSpec((tm // 2, fw), row),
            pl.BlockSpec((None, groups, ch, ch), lambda i: (layer, 0, 0, 0), pipeline_mode=pl.Buffered(1)),
            pl.BlockSpec((None, aw + fw, d), lambda i: (layer, 0, 0), pipeline_mode=pl.Buffered(1)),
        ],
        out_specs=pl.BlockSpec((tm, d), row),
        out_shape=jax.ShapeDtypeStruct((rows, d), F32),
        compiler_params=_cparams(("parallel",)),
        name="mixer_out",
    )(xa, mods, o, y, p["w_f"], p["w_o"])


def _dft_parts(n):
    idx = np.arange(n, dtype=np.int64)
    ang = 2.0 * np.pi * ((idx[:, None] * idx[None, :]) % n).astype(np.float64) / n
    s = 1.0 / math.sqrt(n)
    return np.cos(ang) * s, np.sin(ang) * s


def _tables(seq, ctx_len, ch, n1, n2, tm):
    t = {}
    c, s = _dft_parts(ch)
    t["dft_ch"] = jnp.asarray(np.concatenate([c, -s], axis=1), BF16)
    c, s = _dft_parts(n1)
    t["dft_a"] = jnp.asarray(np.block([[c, s], [-s, c]]), BF16)
    c, s = _dft_parts(n2)
    t["dft_b"] = jnp.asarray(np.concatenate([c, s], axis=1), BF16)
    c, s = _dft_parts(ctx_len)
    t["dft_ctx"] = jnp.asarray(np.concatenate([c, s], axis=1), BF16)
    i2 = np.arange(n2, dtype=np.int64)[:, None]
    k1 = np.arange(n1, dtype=np.int64)[None, :]
    ang = 2.0 * np.pi * ((i2 * k1) % seq).astype(np.float64) / seq
    t["tw_c"] = jnp.asarray(np.broadcast_to(np.cos(ang)[:, :, None], (n2, n1, LANES)), F32)
    t["tw_s"] = jnp.asarray(np.broadcast_to(np.sin(ang)[:, :, None], (n2, n1, LANES)), F32)

    rows = seq // GRID_W
    row = jnp.broadcast_to(jnp.arange(rows, dtype=F32)[:, None], (rows, GRID_W)).reshape(-1)
    col = jnp.broadcast_to(jnp.arange(GRID_W, dtype=F32)[None, :], (rows, GRID_W)).reshape(-1)
    inv_freq = ROPE_THETA ** (-jnp.arange(AXIS_FREQS, dtype=F32) / AXIS_FREQS)
    ang_r = row[:, None] * inv_freq
    ang_c = col[:, None] * inv_freq
    cr, sr, cc, sc = jnp.cos(ang_r), jnp.sin(ang_r), jnp.cos(ang_c), jnp.sin(ang_c)
    cos = jnp.concatenate([cr, cr, cc, cc], axis=1)
    sin = jnp.concatenate([-sr, sr, -sc, sc], axis=1)
    cos = jnp.concatenate([cos, jnp.ones((tm, QK_ROPE_DIM), F32)], axis=0)
    sin = jnp.concatenate([sin, jnp.zeros((tm, QK_ROPE_DIM), F32)], axis=0)
    pad = ((0, 0), (0, LANES - QK_ROPE_DIM))
    t["rope_ck"] = jnp.pad(cos, pad)
    t["rope_sk"] = jnp.pad(sin, pad)
    t["rope_cqt"] = cos.T
    t["rope_sqt"] = sin.T
    return t


def _swap_perm():
    a = AXIS_FREQS
    return np.concatenate([np.arange(a, 2 * a), np.arange(0, a), np.arange(3 * a, 4 * a), np.arange(2 * a, 3 * a)])


def kernel(x, c, ctx, c_ctx, w_ada, b_ada, norm_g, ffn1_w_gate, ffn1_w_up, ffn1_w_down, ffn2_w_gate, ffn2_w_up,
           ffn2_w_down, w_in, q_lat_g, w_uq, kv_lat_g, w_ukv, q_norm_g, k_norm_g, w_fourier, w_o):
    batch, seq, d = x.shape
    ctx_len = ctx.shape[1]
    depth = w_ada.shape[0]
    rq = q_lat_g.shape[1]
    rkv = kv_lat_g.shape[1]
    heads = w_uq.shape[2] // QK_DIM
    groups = w_fourier.shape[1]
    ch = w_fourier.shape[2]
    fw = groups * ch
    tm = ROW_TILE
    lat_rows = batch * seq
    ctx_rows = batch * ctx_len
    rows = lat_rows + ctx_rows
    n1 = 1 << ((seq.bit_length() - 1) // 2)
    n2 = seq // n1
    assert n1 * n2 == seq and n1 % DFT_BATCH == 0 and n2 % DFT_BATCH == 0
    assert ctx_len == KEY_CHUNK and seq % tm == 0 and ctx_rows % tm == 0 and ctx_rows <= seq
    assert rows % n1 == 0 and rows % n2 == 0 and seq % ATT_TQ == 0 and seq % (2 * ATT_TK) == 0
    assert batch + 1 <= COND_ROWS and w_in.shape[2] == rq + rkv + QK_ROPE_DIM + fw
    assert w_ukv.shape[2] == heads * (QK_NOPE_DIM + V_HEAD_DIM)

    perm = _swap_perm()
    k0 = rq + rkv
    zpad = jnp.zeros((depth, d, LANES - QK_ROPE_DIM), w_in.dtype)
    w_rope = w_in[:, :, k0:k0 + QK_ROPE_DIM]
    w_in_ext = jnp.concatenate(
        [w_in[:, :, :k0], w_in[:, :, k0 + QK_ROPE_DIM:], w_rope, zpad, w_rope[:, :, perm], zpad], axis=2)
    w_ukv4 = w_ukv.reshape(depth, rkv, heads, QK_NOPE_DIM + V_HEAD_DIM)
    gpad = ((0, 0), (0, 0), (0, LANES - QK_ROPE_DIM))
    g_rope = k_norm_g[:, None, QK_NOPE_DIM:]
    p = _tables(seq, ctx_len, ch, n1, n2, tm)
    p.update(
        w_in=w_in_ext.astype(BF16),
        q_lat_g=q_lat_g[:, None, :],
        w_uqt=jnp.swapaxes(w_uq, 1, 2).astype(BF16),
        gq=q_norm_g[:, :, None],
        kv_lat_g=kv_lat_g[:, None, :],
        w_uk=w_ukv4[..., :QK_NOPE_DIM].reshape(depth, rkv, heads * QK_NOPE_DIM).astype(BF16),
        w_uvt=jnp.swapaxes(w_ukv4[..., QK_NOPE_DIM:].reshape(depth, rkv, heads * V_HEAD_DIM), 1, 2).astype(BF16),
        gk_n=k_norm_g[:, None, :QK_NOPE_DIM],
        gk_r=jnp.pad(g_rope, gpad),
        gk_rs=jnp.pad(g_rope[:, :, perm], gpad),
        w_f=w_fourier.astype(BF16),
        w_o=w_o.astype(BF16),
    )
    ffn1 = tuple(w.astype(BF16) for w in (ffn1_w_gate, ffn1_w_up, ffn1_w_down))
    ffn2 = tuple(w.astype(BF16) for w in (ffn2_w_gate, ffn2_w_up, ffn2_w_down))

    cond = jnp.concatenate([c, c_ctx[None, :], jnp.zeros((COND_ROWS - batch - 1, d), c.dtype)], axis=0)
    mods_all = _adaln(cond, w_ada, b_ada).reshape(depth, COND_ROWS, N_MOD, d)

    dims = (heads, rq, rkv, fw, groups)
    for layer in range(depth):
        last = layer == depth - 1
        mods = mods_all[layer]
        g = norm_g[layer]
        if layer == 0:
            xa = _ffn(x.reshape(lat_rows, d), mods, g[0:1], *ffn1, layer, mod0=0, rows=lat_rows, seq=seq,
                      out_rows=rows)
            xa = _ffn(ctx.reshape(ctx_rows, d), mods, g[0:1], *ffn1, layer, mod0=0, rows=ctx_rows, seq=seq,
                      out_rows=rows, tile0=lat_rows // tm, into=xa)
        else:
            xa = _ffn(xa, mods, g[0:1], *ffn1, layer, mod0=0, rows=rows, seq=seq)
        qt, k, vt, zr, zi = _mixer_in(xa, mods, g[1:2], p, layer, seq=seq, lat_rows=lat_rows, dims=dims)
        z2 = _dft_a(zr, zi, p, batch=batch, n1=n1, n2=n2, fw=fw)
        y = _dft_b(z2, p, rows=rows, batch=batch, n1=n1, n2=n2, fw=fw)
        o = _attention(qt, k, vt, batch=batch, seq=seq, ctx_len=ctx_len, heads=heads)
        if not last:
            o, y = _ctx_mixer(qt, k, vt, zr, zi, o, y, p, batch=batch, lat_rows=lat_rows, ctx_len=ctx_len,
                              heads=heads, fw=fw)
        out_rows = lat_rows if last else rows
        xa = _mixer_out(xa, mods, o, y, p, layer, rows=out_rows, seq=seq, groups=groups)
        xa = _ffn(xa, mods, g[2:3], *ffn2, layer, mod0=6, rows=out_rows, seq=seq)
    return xa.reshape(batch, seq, d)
```
